```python
import jax, jax.numpy as jnp
from jax import lax
import numpy as np

D_MODEL = 1024
BATCH = 4
SEQ = 8192
DEPTH = 1

M_HEADS = 4
M_QK_DIM = 256
M_V_DIM = 512
M_CHUNK = 128
CONV_W = 4
G_HEADS = 8
G_K_DIM = 128
G_V_DIM = 128
G_CHUNK = 64
P_HEADS = 8
P_NKEYS = 128
P_EXPERTS = P_NKEYS * P_NKEYS
P_QDIM = 256
P_TOPK = 16
P_TOKEN_BLOCK = 128
EPS = 1e-6

M_QK = M_HEADS * M_QK_DIM
M_V = M_HEADS * M_V_DIM
G_K = G_HEADS * G_K_DIM
G_V = G_HEADS * G_V_DIM
IN_SPLITS = (2 * M_QK, M_V, M_V, M_HEADS, M_HEADS, G_K, G_V, G_K, G_V, D_MODEL, D_MODEL)
IN_COLS = sum(IN_SPLITS)
MF_OFFSET = 2 * M_QK + 2 * M_V + M_HEADS

kernel_name = "hybrid_mlstm_hgrn2_peer_block"


def rmsnorm(x, g):
    xf = x.astype(jnp.float32)
    y = xf * lax.rsqrt(jnp.mean(xf * xf, axis=-1, keepdims=True) + EPS)
    return (y * g.astype(jnp.float32)).astype(x.dtype)


def head_rmsnorm(h, g, n_heads):
    shp = h.shape
    hf = h.astype(jnp.float32).reshape(shp[:-1] + (n_heads, shp[-1] // n_heads))
    hf = hf * lax.rsqrt(jnp.mean(hf * hf, axis=-1, keepdims=True) + EPS)
    return (hf.reshape(shp) * g.astype(jnp.float32)).astype(h.dtype)


def causal_conv(x, w, b):
    S = x.shape[1]
    xp = jnp.pad(x, ((0, 0), (CONV_W - 1, 0), (0, 0)))
    y = b + xp[:, 0:S] * w[0]
    for j in range(1, CONV_W):
        y = y + xp[:, j:j + S] * w[j]
    return y


def to_chunks(t, L):
    B, S, H = t.shape[:3]
    t = t.reshape((B, S // L, L, H) + t.shape[3:])
    return jnp.moveaxis(t, (1, 3), (0, 2))


def from_chunks(t):
    t = jnp.moveaxis(t, (0, 2), (1, 3))
    B, NC, L, H, d = t.shape
    return t.reshape(B, NC * L, H * d)


def mlstm_chunkwise(q, k, v, i_pre, f_pre):
    B, S, H, dk = q.shape
    dv = v.shape[-1]
    L = M_CHUNK
    f32 = jnp.float32
    qc = to_chunks(q.astype(f32) * (dk ** -0.5), L)
    kc = to_chunks(k.astype(f32), L)
    vc = to_chunks(v.astype(f32), L)
    ic = to_chunks(i_pre.astype(f32), L)
    lfc = to_chunks(jax.nn.log_sigmoid(f_pre.astype(f32)), L)
    causal = jnp.tril(jnp.ones((L, L), dtype=bool))

    def step(carry, inp):
        C, n, m = carry
        qb, kb, vb, ib, lfb = inp
        b = jnp.cumsum(lfb, axis=-1)
        dmat = b[..., :, None] - b[..., None, :] + ib[..., None, :]
        dmat = jnp.where(causal, dmat, -jnp.inf)
        inter = b + m[..., None]
        m_t = jnp.maximum(inter, jnp.max(dmat, axis=-1))
        dexp = jnp.exp(dmat - m_t[..., None])
        inter_w = jnp.exp(inter - m_t)
        scores = jnp.einsum('bhtd,bhsd->bhts', qb, kb) * dexp
        num = (jnp.einsum('bhts,bhsv->bhtv', scores, vb)
               + inter_w[..., None] * jnp.einsum('bhtd,bhdv->bhtv', qb, C))
        den = jnp.sum(scores, axis=-1) + inter_w * jnp.einsum('bhtd,bhd->bht', qb, n)
        h = num / jnp.maximum(jnp.abs(den), jnp.exp(-m_t))[..., None]
        m_new = m_t[..., -1]
        w_s = jnp.exp(b[..., -1:] - b + ib - m_new[..., None])
        decay = jnp.exp(b[..., -1] + m - m_new)
        kw = kb * w_s[..., None]
        C_new = decay[..., None, None] * C + jnp.einsum('bhsd,bhsv->bhdv', kw, vb)
        n_new = decay[..., None] * n + jnp.sum(kw, axis=2)
        return (C_new, n_new, m_new), h

    init = (jnp.zeros((B, H, dk, dv), f32), jnp.zeros((B, H, dk), f32), jnp.zeros((B, H), f32))
    _, h = lax.scan(step, init, (qc, kc, vc, ic, lfc))
    return from_chunks(h).astype(q.dtype)


def hgrn2_chunkwise(q, k, log_f, i):
    B, S, H, dk = q.shape
    dv = i.shape[-1]
    L = G_CHUNK
    f32 = jnp.float32
    qc = to_chunks(q.astype(f32), L)
    kc = to_chunks(k.astype(f32), L)
    gc = to_chunks(log_f.astype(f32), L)
    ic = to_chunks(i.astype(f32), L)
    causal = jnp.tril(jnp.ones((L, L), dtype=bool))

    def step(state, inp):
        qb, kb, gb, ib = inp
        cum = jnp.cumsum(gb, axis=2)
        rel = cum[:, :, :, None, :] - cum[:, :, None, :, :]
        rel = jnp.where(causal[:, :, None], rel, -jnp.inf)
        att = jnp.einsum('bhtd,bhsd,bhtsd->bhts', qb, kb, jnp.exp(rel))
        intra = jnp.einsum('bhts,bhsv->bhtv', att, ib)
        inter = jnp.einsum('bhtd,bhdv->bhtv', qb * jnp.exp(cum), state)
        last = cum[:, :, -1:, :]
        state_new = (jnp.exp(last[:, :, 0])[..., None] * state
                     + jnp.einsum('bhsd,bhsv->bhdv', kb * jnp.exp(last - cum), ib))
        return state_new, intra + inter

    init = jnp.zeros((B, H, dk, dv), f32)
    _, o = lax.scan(step, init, (qc, kc, gc, ic))
    return from_chunks(o).astype(q.dtype)


def mixer_block(u, w_in, b_in, conv_w, conv_b, m_head_g, lb, g_head_g, w_br_m, w_br_g, w_out):
    B, S, _ = u.shape
    proj = u @ w_in + b_in
    points = [int(c) for c in np.cumsum(IN_SPLITS)[:-1]]
    mqk, mv, mo, mi, mf, gf, gi, gq, gg, a_m, a_g = jnp.split(proj, points, axis=-1)

    qk = jax.nn.silu(causal_conv(mqk, conv_w, conv_b))
    mq = qk[..., :M_QK].reshape(B, S, M_HEADS, M_QK_DIM)
    mk = qk[..., M_QK:].reshape(B, S, M_HEADS, M_QK_DIM)
    h_m = mlstm_chunkwise(mq, mk, mv.reshape(B, S, M_HEADS, M_V_DIM), mi, mf)
    y_m = jax.nn.sigmoid(mo) * head_rmsnorm(h_m, m_head_g, M_HEADS)

    lb_h = lb.astype(jnp.float32).reshape(G_HEADS, G_K_DIM)
    f = lb_h + (1.0 - lb_h) * jax.nn.sigmoid(gf.astype(jnp.float32).reshape(B, S, G_HEADS, G_K_DIM))
    o_g = hgrn2_chunkwise(jax.nn.silu(gq).reshape(B, S, G_HEADS, G_K_DIM), 1.0 - f, jnp.log(f),
                          gi.reshape(B, S, G_HEADS, G_V_DIM))
    y_g = head_rmsnorm(o_g, g_head_g, G_HEADS) * jax.nn.silu(gg)

    z = jax.nn.sigmoid(a_m) * (y_m @ w_br_m) + jax.nn.sigmoid(a_g) * (y_g @ w_br_g)
    return z @ w_out


def peer_ffn(x, w_pq, sub_keys, u_emb, v_emb):
    B, S, D = x.shape
    T = B * S
    half = P_QDIM // 2

    def block(xb):
        tb = xb.shape[0]
        q = (xb @ w_pq).reshape(tb, P_HEADS, 2, half)
        s = jnp.einsum('thpc,hpnc->thpn', q, sub_keys).astype(jnp.float32)
        sv, si = lax.top_k(s, P_TOPK)
        cand = (sv[:, :, 0, :, None] + sv[:, :, 1, None, :]).reshape(tb, P_HEADS, P_TOPK * P_TOPK)
        cidx = (si[:, :, 0, :, None] * P_NKEYS + si[:, :, 1, None, :]).reshape(tb, P_HEADS, P_TOPK * P_TOPK)
        top_s, pos = lax.top_k(cand, P_TOPK)
        eidx = jnp.take_along_axis(cidx, pos, axis=-1)
        gate = jax.nn.softmax(top_s, axis=-1).astype(xb.dtype)
        act = jax.nn.gelu(jnp.einsum('thkd,td->thk', u_emb[eidx], xb), approximate=False)
        return jnp.einsum('thk,thkd->td', gate * act, v_emb[eidx])

    y = lax.map(block, x.reshape(T // P_TOKEN_BLOCK, P_TOKEN_BLOCK, D))
    return y.reshape(B, S, D)


def setup_inputs(seed: int = 0) -> dict:
    key = jax.random.key(seed)
    ks = jax.random.split(key, 20)

    def nrm(k, shape, scale):
        return jax.random.normal(k, shape, jnp.float32) * scale

    x = nrm(ks[0], (BATCH, SEQ, D_MODEL), 1.0)
    norm1_g = 1.0 + nrm(ks[1], (DEPTH, D_MODEL), 0.02)
    w_in = nrm(ks[2], (DEPTH, D_MODEL, IN_COLS), D_MODEL ** -0.5)
    b_in = nrm(ks[3], (DEPTH, IN_COLS), 0.01)
    b_in = b_in.at[:, MF_OFFSET:MF_OFFSET + M_HEADS].add(jnp.linspace(3.0, 6.0, M_HEADS))
    conv_w = nrm(ks[4], (DEPTH, CONV_W, 2 * M_QK), CONV_W ** -0.5)
    conv_b = nrm(ks[5], (DEPTH, 2 * M_QK), 0.01)
    m_head_g = 1.0 + nrm(ks[6], (DEPTH, M_V), 0.02)
    lb_table = nrm(ks[7], (DEPTH + 1, G_K), 0.1)
    g_head_g = 1.0 + nrm(ks[8], (DEPTH, G_V), 0.02)
    w_br_m = nrm(ks[9], (DEPTH, M_V, D_MODEL), M_V ** -0.5)
    w_br_g = nrm(ks[10], (DEPTH, G_V, D_MODEL), G_V ** -0.5)
    w_out = nrm(ks[11], (DEPTH, D_MODEL, D_MODEL), D_MODEL ** -0.5)
    norm2_g = 1.0 + nrm(ks[12], (DEPTH, D_MODEL), 0.02)
    w_pq = nrm(ks[13], (DEPTH, D_MODEL, P_HEADS * P_QDIM), D_MODEL ** -0.5)
    sub_keys = nrm(ks[14], (DEPTH, P_HEADS, 2, P_NKEYS, P_QDIM // 2), (P_QDIM // 2) ** -0.5)
    u_emb = nrm(ks[15], (DEPTH, P_EXPERTS, D_MODEL), D_MODEL ** -0.5)
    v_emb = nrm(ks[16], (DEPTH, P_EXPERTS, D_MODEL), 0.5)
    final_g = 1.0 + nrm(ks[17], (D_MODEL,), 0.02)
    return {"x": x, "norm1_g": norm1_g, "w_in": w_in, "b_in": b_in, "conv_w": conv_w,
            "conv_b": conv_b, "m_head_g": m_head_g, "lb_table": lb_table, "g_head_g": g_head_g,
            "w_br_m": w_br_m, "w_br_g": w_br_g, "w_out": w_out, "norm2_g": norm2_g,
            "w_pq": w_pq, "sub_keys": sub_keys, "u_emb": u_emb, "v_emb": v_emb, "final_g": final_g}


def reference(x, norm1_g, w_in, b_in, conv_w, conv_b, m_head_g, lb_table, g_head_g,
              w_br_m, w_br_g, w_out, norm2_g, w_pq, sub_keys, u_emb, v_emb, final_g):
    lb_all = jnp.cumsum(jax.nn.softmax(lb_table.astype(jnp.float32), axis=0), axis=0)
    for l in range(DEPTH):
        x = x + mixer_block(rmsnorm(x, norm1_g[l]), w_in[l], b_in[l], conv_w[l], conv_b[l],
                            m_head_g[l], lb_all[l], g_head_g[l], w_br_m[l], w_br_g[l], w_out[l])
        x = x + peer_ffn(rmsnorm(x, norm2_g[l]), w_pq[l], sub_keys[l], u_emb[l], v_emb[l])
    return rmsnorm(x, final_g)
```

```python
import functools

import jax
import jax.numpy as jnp
from jax import lax
from jax.experimental import pallas as pl
from jax.experimental.pallas import tpu as pltpu

F32 = jnp.float32
BF16 = jnp.bfloat16
HIGHEST = lax.Precision.HIGHEST

EPS = 1e-6
LANES = 128
SUBLANES = 8
VMEM_LIMIT = 48 * 1024 * 1024

M_HEADS, M_QK_DIM, M_V_DIM, M_CHUNK, CONV_W = 4, 256, 512, 128, 4
G_HEADS, G_DIM, G_CHUNK, G_SUB = 8, 128, 64, 16
P_HEADS, P_NKEYS, P_TOPK = 8, 128, 16
M_QK = M_HEADS * M_QK_DIM
M_V = M_HEADS * M_V_DIM
G_W = G_HEADS * G_DIM


def _params(*sem):
    return pltpu.CompilerParams(dimension_semantics=sem, vmem_limit_bytes=VMEM_LIMIT)


def _dot(a, b):
    return jnp.dot(a, b, preferred_element_type=F32)


def _dot_nt(a, b):
    return lax.dot_general(a, b, (((1,), (1,)), ((), ())), preferred_element_type=F32)


def _dot_tn(a, b):
    return lax.dot_general(a, b, (((0,), (0,)), ((), ())), preferred_element_type=F32)


def _sigmoid(x):
    return 1.0 / (1.0 + jnp.exp(-x))


def _tril(n):
    r = lax.broadcasted_iota(jnp.int32, (n, n), 0)
    c = lax.broadcasted_iota(jnp.int32, (n, n), 1)
    return (r >= c)


def _inproj_kernel(x_ref, g_ref, w_ref, b_ref, wg_ref, bg_ref, out_ref, gate_ref, u_scr):
    @pl.when(pl.program_id(1) == 0)
    def _():
        x = x_ref[...]
        u = x * lax.rsqrt(jnp.mean(x * x, axis=-1, keepdims=True) + EPS) * g_ref[...]
        u_scr[...] = u.astype(BF16)
        gate_ref[...] = jnp.dot(u, wg_ref[...], precision=HIGHEST,
                                preferred_element_type=F32) + bg_ref[...]

    out_ref[...] = (_dot(u_scr[...], w_ref[...]) + b_ref[...]).astype(BF16)


def _inproj(x2, g, w, b, wg, bg, tm, tn):
    t, d = x2.shape
    n = w.shape[1]
    return pl.pallas_call(
        _inproj_kernel,
        out_shape=(jax.ShapeDtypeStruct((t, n), BF16), jax.ShapeDtypeStruct((t, LANES), F32)),
        grid=(t // tm, n // tn),
        in_specs=[
            pl.BlockSpec((tm, d), lambda i, j: (i, 0)),
            pl.BlockSpec((1, d), lambda i, j: (0, 0)),
            pl.BlockSpec((d, tn), lambda i, j: (0, j)),
            pl.BlockSpec((1, tn), lambda i, j: (0, j)),
            pl.BlockSpec((d, LANES), lambda i, j: (0, 0)),
            pl.BlockSpec((1, LANES), lambda i, j: (0, 0)),
        ],
        out_specs=(pl.BlockSpec((tm, tn), lambda i, j: (i, j)),
                   pl.BlockSpec((tm, LANES), lambda i, j: (i, 0))),
        scratch_shapes=[pltpu.VMEM((tm, d), BF16)],
        compiler_params=_params("parallel", "arbitrary"),
        name="inproj",
    )(x2, g, w, b, wg, bg)


def _mlstm_kernel(qk_ref, v_ref, o_ref, gt_ref, cw_ref, cb_ref, hg_ref, y_ref,
                  ext_scr, c_scr, m_scr):
    L = M_CHUNK

    @pl.when(pl.program_id(1) == 0)
    def _():
        ext_scr[0:SUBLANES, :] = jnp.zeros((SUBLANES, 2 * M_QK), F32)
        c_scr[...] = jnp.zeros(c_scr.shape, F32)
        m_scr[...] = jnp.zeros(m_scr.shape, F32)

    cur = qk_ref[...].astype(F32)
    ext_scr[SUBLANES:SUBLANES + L, :] = cur
    acc = cb_ref[...] + cw_ref[CONV_W - 1:CONV_W, :] * cur
    for j in range(CONV_W - 1):
        off = SUBLANES - (CONV_W - 1) + j
        acc = acc + cw_ref[j:j + 1, :] * ext_scr[off:off + L, :]
    ext_scr[0:SUBLANES, :] = cur[L - SUBLANES:L, :]
    qk = acc * _sigmoid(acc)

    gt = gt_ref[...]
    lf = jnp.minimum(gt, 0.0) - jnp.log(1.0 + jnp.exp(-jnp.abs(gt)))
    tri = _tril(L)
    bcols = jnp.dot(tri.astype(F32), lf, precision=HIGHEST, preferred_element_type=F32)
    gt_t = gt.T
    b_t = bcols.T
    ones_ext = jnp.ones((L, LANES), BF16)

    for h in range(M_HEADS):
        icol = gt[:, h:h + 1]
        bcol = bcols[:, M_HEADS + h:M_HEADS + h + 1]
        irow = gt_t[h:h + 1, :]
        brow = b_t[M_HEADS + h:M_HEADS + h + 1, :]
        m_prev = m_scr[h, 0:1, 0:1]
        b_last = bcol[L - 1:L, :]

        dmat = jnp.where(tri, bcol - brow + irow, -jnp.inf)
        inter = bcol + m_prev
        m_t = jnp.maximum(inter, jnp.max(dmat, axis=-1, keepdims=True))
        dexp = jnp.exp(dmat - m_t)
        inter_w = jnp.exp(inter - m_t)

        q = (qk[:, h * M_QK_DIM:(h + 1) * M_QK_DIM] * (M_QK_DIM ** -0.5)).astype(BF16)
        kf = qk[:, M_QK + h * M_QK_DIM:M_QK + (h + 1) * M_QK_DIM]
        v_ext = jnp.concatenate([v_ref[:, h * M_V_DIM:(h + 1) * M_V_DIM], ones_ext], axis=1)

        scores = (_dot_nt(q, kf.astype(BF16)) * dexp).astype(BF16)
        c_old = c_scr[h]
        nd = _dot(scores, v_ext) + inter_w * _dot(q, c_old.astype(BF16))
        den = nd[:, M_V_DIM:]
        inv = 1.0 / jnp.maximum(jnp.abs(den), jnp.exp(-m_t))
        hs = [nd[:, j * LANES:(j + 1) * LANES] * inv for j in range(M_V_DIM // LANES)]
        hh = jnp.concatenate(hs, axis=1)

        m_new = m_t[L - 1:L, :]
        w_s = jnp.exp(b_last - bcol + icol - m_new)
        decay = jnp.exp(b_last + m_prev - m_new)
        kw = (kf * w_s).astype(BF16)
        c_scr[h] = decay * c_old + _dot_tn(kw, v_ext)
        m_scr[h] = jnp.broadcast_to(m_new, (SUBLANES, LANES))

        hn = hh * lax.rsqrt(jnp.mean(hh * hh, axis=-1, keepdims=True) + EPS)
        sl = slice(h * M_V_DIM, (h + 1) * M_V_DIM)
        og = _sigmoid(o_ref[:, sl].astype(F32))
        y_ref[:, sl] = (og * hn * hg_ref[:, sl]).astype(BF16)


def _mlstm(proj, gates, conv_w, conv_b, head_g, nb, nc):
    t = proj.shape[0]
    L = M_CHUNK
    row = lambda b, c: b * nc + c
    return pl.pallas_call(
        _mlstm_kernel,
        out_shape=jax.ShapeDtypeStruct((t, M_V), BF16),
        grid=(nb, nc),
        in_specs=[
            pl.BlockSpec((L, 2 * M_QK), lambda b, c: (row(b, c), 0)),
            pl.BlockSpec((L, M_V), lambda b, c: (row(b, c), 1)),
            pl.BlockSpec((L, M_V), lambda b, c: (row(b, c), 2)),
            pl.BlockSpec((L, LANES), lambda b, c: (row(b, c), 0)),
            pl.BlockSpec((CONV_W, 2 * M_QK), lambda b, c: (0, 0)),
            pl.BlockSpec((1, 2 * M_QK), lambda b, c: (0, 0)),
            pl.BlockSpec((1, M_V), lambda b, c: (0, 0)),
        ],
        out_specs=pl.BlockSpec((L, M_V), lambda b, c: (row(b, c), 0)),
        scratch_shapes=[
            pltpu.VMEM((SUBLANES + L, 2 * M_QK), F32),
            pltpu.VMEM((M_HEADS, M_QK_DIM, M_V_DIM + LANES), F32),
            pltpu.VMEM((M_HEADS, SUBLANES, LANES), F32),
        ],
        compiler_params=_params("parallel", "arbitrary"),
        name="mlstm",
    )(proj, proj, proj, gates, conv_w, conv_b, head_g)


def _hgrn_kernel(gf_ref, gi_ref, gq_ref, gg_ref, lb_ref, hg_ref, y_ref, st_scr):
    L, SUB, NSUB = G_CHUNK, G_SUB, G_CHUNK // G_SUB

    @pl.when(pl.program_id(1) == 0)
    def _():
        st_scr[...] = jnp.zeros(st_scr.shape, F32)

    lb = lb_ref[...]
    f = lb + (1.0 - lb) * _sigmoid(gf_ref[...].astype(F32))
    k_all = 1.0 - f
    cum_all = jnp.dot(_tril(L).astype(F32), jnp.log(f), precision=HIGHEST,
                      preferred_element_type=F32)
    gq = gq_ref[...].astype(F32)
    q_all = gq * _sigmoid(gq)

    row = lax.broadcasted_iota(jnp.int32, (L, G_DIM), 0)
    row_sub = lax.broadcasted_iota(jnp.int32, (SUB, G_DIM), 0)
    ones_sq = jnp.ones((G_DIM, G_DIM), BF16)

    for h in range(G_HEADS):
        sl = slice(h * G_DIM, (h + 1) * G_DIM)
        cum, q, k = cum_all[:, sl], q_all[:, sl], k_all[:, sl]
        iv = gi_ref[:, sl]
        ivf = iv.astype(F32)
        ends = [cum[SUB * b + SUB - 1:SUB * b + SUB, :] for b in range(NSUB)]
        last = ends[NSUB - 1]
        end_of_row = jnp.concatenate([jnp.broadcast_to(e, (SUB, G_DIM)) for e in ends], axis=0)

        khat = k * jnp.exp(end_of_row - cum)
        qs, ks = [], []
        for b in range(NSUB - 1):
            lo = SUB * (b + 1)
            qs.append(jnp.where(row >= lo, q * jnp.exp(jnp.minimum(cum - ends[b], 0.0)), 0.0))
            ks.append(jnp.where((row >= lo - SUB) & (row < lo), khat, 0.0))
        att = _dot_nt(jnp.concatenate(qs, axis=1).astype(BF16),
                      jnp.concatenate(ks, axis=1).astype(BF16))
        o = _dot(att.astype(BF16), iv)

        st = st_scr[h]
        o = o + _dot_nt((q * jnp.exp(cum)).astype(BF16), st.astype(BF16))

        o_diag = []
        for a in range(NSUB):
            r0 = SUB * a
            q_a, k_a, c_a, i_a = (z[r0:r0 + SUB, :] for z in (q, k, cum, ivf))
            ms = []
            for s in range(SUB):
                e = jnp.exp(jnp.where(row_sub >= s, c_a - c_a[s:s + 1, :], -jnp.inf))
                ms.append(q_a * (k_a[s:s + 1, :] * e))
            red = _dot(jnp.concatenate(ms, axis=0).astype(BF16), ones_sq)
            acc = red[0:SUB, :] * i_a[0:1, :]
            for s in range(1, SUB):
                acc = acc + red[s * SUB:(s + 1) * SUB, :] * i_a[s:s + 1, :]
            o_diag.append(acc)
        o = o + jnp.concatenate(o_diag, axis=0)

        kd = (k * jnp.exp(last - cum)).astype(BF16)
        st_scr[h] = jnp.exp(last) * st + _dot_tn(iv, kd)

        hn = o * lax.rsqrt(jnp.mean(o * o, axis=-1, keepdims=True) + EPS)
        gg = gg_ref[:, sl].astype(F32)
        y_ref[:, sl] = (hn * hg_ref[:, sl] * (gg * _sigmoid(gg))).astype(BF16)


def _hgrn(proj, lb, head_g, nb, nc, col0):
    t = proj.shape[0]
    L = G_CHUNK
    row = lambda b, c: b * nc + c
    spec = lambda j: pl.BlockSpec((L, G_W), lambda b, c: (row(b, c), col0 + j))
    return pl.pallas_call(
        _hgrn_kernel,
        out_shape=jax.ShapeDtypeStruct((t, G_W), BF16),
        grid=(nb, nc),
        in_specs=[spec(0), spec(1), spec(2), spec(3),
                  pl.BlockSpec((1, G_W), lambda b, c: (0, 0)),
                  pl.BlockSpec((1, G_W), lambda b, c: (0, 0))],
        out_specs=pl.BlockSpec((L, G_W), lambda b, c: (row(b, c), 0)),
        scratch_shapes=[pltpu.VMEM((G_HEADS, G_DIM, G_DIM), F32)],
        compiler_params=_params("parallel", "arbitrary"),
        name="hgrn2",
    )(proj, proj, proj, proj, lb, head_g)


def _merge_kernel(x_ref, ym_ref, yg_ref, am_ref, ag_ref, wm_ref, wg_ref, wo_ref, g2_ref,
                  x1_ref, u2_ref):
    zm = _dot(ym_ref[...], wm_ref[...])
    zg = _dot(yg_ref[...], wg_ref[...])
    z = _sigmoid(am_ref[...].astype(F32)) * zm + _sigmoid(ag_ref[...].astype(F32)) * zg
    x1 = x_ref[...] + _dot(z.astype(BF16), wo_ref[...])
    x1_ref[...] = x1
    u2 = x1 * lax.rsqrt(jnp.mean(x1 * x1, axis=-1, keepdims=True) + EPS) * g2_ref[...]
    u2_ref[...] = u2.astype(BF16)


def _merge(x2, ym, yg, proj, wm, wg, wo, g2, tm, col_am):
    t, d = x2.shape
    rows = lambda w: pl.BlockSpec((tm, w), lambda i: (i, 0))
    full = lambda a: pl.BlockSpec(a.shape, lambda i: (0, 0))
    return pl.pallas_call(
        _merge_kernel,
        out_shape=(jax.ShapeDtypeStruct((t, d), F32), jax.ShapeDtypeStruct((t, d), BF16)),
        grid=(t // tm,),
        in_specs=[rows(d), rows(M_V), rows(G_W),
                  pl.BlockSpec((tm, d), lambda i: (i, col_am)),
                  pl.BlockSpec((tm, d), lambda i: (i, col_am + 1)),
                  full(wm), full(wg), full(wo), full(g2)],
        out_specs=(rows(d), rows(d)),
        compiler_params=_params("parallel"),
        name="merge",
    )(x2, ym, yg, proj, proj, wm, wg, wo, g2)


def _top_rows(vals, k):
    out = []
    for _ in range(k):
        m = jnp.max(vals, axis=0, keepdims=True)
        out.append(m)
        vals = jnp.where(vals == m, -jnp.inf, vals)
    return out


def _route_kernel(u2_ref, wq_ref, keys_ref, thr_ref, w0_ref, s1_ref, e1_ref, cand_scr):
    K = P_TOPK
    tb = u2_ref.shape[0]
    q_t = _dot_nt(wq_ref[...], u2_ref[...])
    pairs = [(r, c) for r in range(K) for c in range(K) if (r + 1) * (c + 1) <= K]
    n_pad = cand_scr.shape[0] - len(pairs)
    cand_scr[len(pairs):, :] = jnp.full((n_pad, tb), -jnp.inf, F32)

    for h in range(P_HEADS):
        s = []
        for p in range(2):
            r0 = (2 * h + p) * P_NKEYS
            s.append(_dot(keys_ref[2 * h + p], q_t[r0:r0 + P_NKEYS, :].astype(BF16)))
        a = _top_rows(s[0], K)
        b = _top_rows(s[1], K)
        for n, (r, c) in enumerate(pairs):
            cand_scr[n:n + 1, :] = a[r] + b[c]
        top = _top_rows(cand_scr[...], K)
        z = jnp.ones_like(top[0])
        for r in range(1, K):
            z = z + jnp.exp(top[r] - top[0])
        thr_ref[h] = top[K - 1] - s[0]
        w0_ref[h] = jnp.exp(s[0] - a[0]) / z
        s1_ref[h] = s[1]
        e1_ref[h] = jnp.exp(s[1] - b[0])


def _route(u2, wq_t, keys, tb):
    t, d = u2.shape
    n_pairs = sum(1 for r in range(P_TOPK) for c in range(P_TOPK) if (r + 1) * (c + 1) <= P_TOPK)
    n_cand = -(-n_pairs // SUBLANES) * SUBLANES
    out = jax.ShapeDtypeStruct((P_HEADS, P_NKEYS, t), F32)
    ospec = pl.BlockSpec((P_HEADS, P_NKEYS, tb), lambda i: (0, 0, i))
    return pl.pallas_call(
        _route_kernel,
        out_shape=(out, out, out, out),
        grid=(t // tb,),
        in_specs=[pl.BlockSpec((tb, d), lambda i: (i, 0)),
                  pl.BlockSpec(wq_t.shape, lambda i: (0, 0)),
                  pl.BlockSpec(keys.shape, lambda i: (0, 0, 0))],
        out_specs=(ospec, ospec, ospec, ospec),
        scratch_shapes=[pltpu.VMEM((n_cand, tb), F32)],
        compiler_params=_params("parallel"),
        name="route",
    )(u2, wq_t, keys)


def _peer_kernel(u2_ref, x1_ref, fg_ref, u_ref, vt_ref, thr_ref, w0_ref, s1_ref, e1_ref,
                 out_ref, acc_scr, p_scr):
    e_blk = pl.program_id(1)

    @pl.when(e_blk == 0)
    def _():
        acc_scr[...] = jnp.zeros(acc_scr.shape, F32)

    h_t = _dot_nt(u_ref[...], u2_ref[...])
    n_i = u_ref.shape[0] // P_NKEYS
    for ii in range(n_i):
        rows = slice(ii * P_NKEYS, (ii + 1) * P_NKEYS)
        hv = h_t[rows, :]
        act = 0.5 * hv * (1.0 + lax.erf(hv * (2.0 ** -0.5)))
        gate = None
        for h in range(P_HEADS):
            sel = jnp.where(s1_ref[h] >= thr_ref[h, ii:ii + 1, :], e1_ref[h], 0.0)
            term = sel * w0_ref[h, ii:ii + 1, :]
            gate = term if gate is None else gate + term
        p_scr[rows, :] = (act * gate).astype(BF16)
    acc_scr[...] += _dot(vt_ref[...], p_scr[...])

    @pl.when(e_blk == pl.num_programs(1) - 1)
    def _():
        x2 = x1_ref[...] + acc_scr[...].T
        out_ref[...] = (x2 * lax.rsqrt(jnp.mean(x2 * x2, axis=-1, keepdims=True) + EPS)
                        * fg_ref[...])


def _peer(u2, x1, fg, u_emb, v_t, thr, w0, s1, e1, tb, eb):
    t, d = u2.shape
    n_exp = u_emb.shape[0]
    n_i = eb // P_NKEYS
    tok = lambda i, e: (i, 0)
    return pl.pallas_call(
        _peer_kernel,
        out_shape=jax.ShapeDtypeStruct((t, d), F32),
        grid=(t // tb, n_exp // eb),
        in_specs=[pl.BlockSpec((tb, d), tok),
                  pl.BlockSpec((tb, d), tok),
                  pl.BlockSpec((1, d), lambda i, e: (0, 0)),
                  pl.BlockSpec((eb, d), lambda i, e: (e, 0)),
                  pl.BlockSpec((d, eb), lambda i, e: (0, e)),
                  pl.BlockSpec((P_HEADS, n_i, tb), lambda i, e: (0, e, i)),
                  pl.BlockSpec((P_HEADS, n_i, tb), lambda i, e: (0, e, i)),
                  pl.BlockSpec((P_HEADS, P_NKEYS, tb), lambda i, e: (0, 0, i)),
                  pl.BlockSpec((P_HEADS, P_NKEYS, tb), lambda i, e: (0, 0, i))],
        out_specs=pl.BlockSpec((tb, d), tok),
        scratch_shapes=[pltpu.VMEM((d, tb), F32), pltpu.VMEM((eb, tb), BF16)],
        compiler_params=_params("parallel", "arbitrary"),
        name="peer",
    )(u2, x1, fg, u_emb, v_t, thr, w0, s1, e1)


def _tiles(t):
    pick = lambda pref: next(c for c in (pref, 512, 256, 128) if c <= pref and t % c == 0)
    return dict(inproj_tm=pick(1024), inproj_tn=1024, merge_tm=pick(512),
                route_tb=pick(256), peer_tb=pick(512), peer_eb=1024)


def kernel(x, norm1_g, w_in, b_in, conv_w, conv_b, m_head_g, lb_table, g_head_g, w_br_m, w_br_g,
           w_out, norm2_g, w_pq, sub_keys, u_emb, v_emb, final_g):
    nb, seq, d = x.shape
    t = nb * seq
    depth = w_in.shape[0]
    assert d == 1024 and seq % M_CHUNK == 0 and t % LANES == 0
    tl = _tiles(t)
    n_gate = 2 * M_HEADS
    g0 = 2 * M_QK + 2 * M_V
    assert (w_in.shape[2] - n_gate) % tl["inproj_tn"] == 0

    lb_all = jnp.cumsum(jax.nn.softmax(lb_table.astype(F32), axis=0), axis=0)
    x2 = x.reshape(t, d)
    row = lambda a: a.reshape(1, -1).astype(F32)

    for l in range(depth):
        w_main = jnp.concatenate([w_in[l][:, :g0], w_in[l][:, g0 + n_gate:]], axis=1).astype(BF16)
        b_main = jnp.concatenate([b_in[l][:g0], b_in[l][g0 + n_gate:]]).reshape(1, -1)
        w_gate = jnp.pad(w_in[l][:, g0:g0 + n_gate], ((0, 0), (0, LANES - n_gate)))
        b_gate = jnp.pad(b_in[l][g0:g0 + n_gate], (0, LANES - n_gate)).reshape(1, -1)

        proj, gates = _inproj(x2, row(norm1_g[l]), w_main, b_main, w_gate, b_gate,
                              tl["inproj_tm"], tl["inproj_tn"])
        y_m = _mlstm(proj, gates, conv_w[l], row(conv_b[l]), row(m_head_g[l]),
                     nb, seq // M_CHUNK)
        y_g = _hgrn(proj, row(lb_all[l]), row(g_head_g[l]), nb, seq // G_CHUNK, g0 // G_W)
        x1, u2 = _merge(x2, y_m, y_g, proj, w_br_m[l].astype(BF16), w_br_g[l].astype(BF16),
                        w_out[l].astype(BF16), row(norm2_g[l]), tl["merge_tm"],
                        (g0 + 4 * G_W) // d)

        keys = sub_keys[l].reshape(2 * P_HEADS, P_NKEYS, -1).astype(BF16)
        thr, w0, s1, e1 = _route(u2, w_pq[l].T.astype(BF16), keys, tl["route_tb"])
        assert depth == 1
        x2 = _peer(u2, x1, row(final_g), u_emb[l].astype(BF16), v_emb[l].T.astype(BF16),
                   thr, w0, s1, e1, tl["peer_tb"], tl["peer_eb"])
    return x2.reshape(nb, seq, d)
```

```python
import functools

import jax
import jax.numpy as jnp
from jax import lax
from jax.experimental import pallas as pl
from jax.experimental.pallas import tpu as pltpu

F32 = jnp.float32
BF16 = jnp.bfloat16
HIGHEST = lax.Precision.HIGHEST

EPS = 1e-6
LANES = 128
SUBLANES = 8
VMEM_LIMIT = 48 * 1024 * 1024

M_HEADS, M_QK_DIM, M_V_DIM, M_CHUNK, CONV_W = 4, 256, 512, 128, 4
G_HEADS, G_DIM, G_CHUNK, G_SUB = 8, 128, 64, 16
P_HEADS, P_NKEYS, P_TOPK = 8, 128, 16
M_QK = M_HEADS * M_QK_DIM
M_V = M_HEADS * M_V_DIM
G_W = G_HEADS * G_DIM


def _params(*sem):
    return pltpu.CompilerParams(dimension_semantics=sem, vmem_limit_bytes=VMEM_LIMIT)


def _dot(a, b):
    return jnp.dot(a, b, preferred_element_type=F32)


def _dot_nt(a, b):
    return lax.dot_general(a, b, (((1,), (1,)), ((), ())), preferred_element_type=F32)


def _dot_tn(a, b):
    return lax.dot_general(a, b, (((0,), (0,)), ((), ())), preferred_element_type=F32)


def _sigmoid(x):
    return 1.0 / (1.0 + jnp.exp(-x))


def _tril(n):
    r = lax.broadcasted_iota(jnp.int32, (n, n), 0)
    c = lax.broadcasted_iota(jnp.int32, (n, n), 1)
    return (r >= c)


def _inproj_kernel(x_ref, g_ref, w_ref, b_ref, wg_ref, bg_ref, out_ref, gate_ref, u_scr):
    @pl.when(pl.program_id(1) == 0)
    def _():
        x = x_ref[...]
        u = x * lax.rsqrt(jnp.mean(x * x, axis=-1, keepdims=True) + EPS) * g_ref[...]
        u_scr[...] = u.astype(BF16)
        gate_ref[...] = jnp.dot(u, wg_ref[...], precision=HIGHEST,
                                preferred_element_type=F32) + bg_ref[...]

    out_ref[...] = (_dot(u_scr[...], w_ref[...]) + b_ref[...]).astype(BF16)


def _inproj(x2, g, w, b, wg, bg, tm, tn):
    t, d = x2.shape
    n = w.shape[1]
    return pl.pallas_call(
        _inproj_kernel,
        out_shape=(jax.ShapeDtypeStruct((t, n), BF16), jax.ShapeDtypeStruct((t, LANES), F32)),
        grid=(t // tm, n // tn),
        in_specs=[
            pl.BlockSpec((tm, d), lambda i, j: (i, 0)),
            pl.BlockSpec((1, d), lambda i, j: (0, 0)),
            pl.BlockSpec((d, tn), lambda i, j: (0, j)),
            pl.BlockSpec((1, tn), lambda i, j: (0, j)),
            pl.BlockSpec((d, LANES), lambda i, j: (0, 0)),
            pl.BlockSpec((1, LANES), lambda i, j: (0, 0)),
        ],
        out_specs=(pl.BlockSpec((tm, tn), lambda i, j: (i, j)),
                   pl.BlockSpec((tm, LANES), lambda i, j: (i, 0))),
        scratch_shapes=[pltpu.VMEM((tm, d), BF16)],
        compiler_params=_params("parallel", "arbitrary"),
        name="inproj",
    )(x2, g, w, b, wg, bg)


def _mlstm_kernel(qk_ref, v_ref, o_ref, gt_ref, cw_ref, cb_ref, hg_ref, y_ref,
                  ext_scr, c_scr, m_scr):
    L = M_CHUNK

    @pl.when(pl.program_id(1) == 0)
    def _():
        ext_scr[0:SUBLANES, :] = jnp.zeros((SUBLANES, 2 * M_QK), F32)
        c_scr[...] = jnp.zeros(c_scr.shape, F32)
        m_scr[...] = jnp.zeros(m_scr.shape, F32)

    cur = qk_ref[...].astype(F32)
    ext_scr[SUBLANES:SUBLANES + L, :] = cur
    acc = cb_ref[...] + cw_ref[CONV_W - 1:CONV_W, :] * cur
    for j in range(CONV_W - 1):
        off = SUBLANES - (CONV_W - 1) + j
        acc = acc + cw_ref[j:j + 1, :] * ext_scr[off:off + L, :]
    ext_scr[0:SUBLANES, :] = cur[L - SUBLANES:L, :]
    qk = acc * _sigmoid(acc)

    gt = gt_ref[...]
    lf = jnp.minimum(gt, 0.0) - jnp.log(1.0 + jnp.exp(-jnp.abs(gt)))
    tri = _tril(L)
    bcols = jnp.dot(tri.astype(F32), lf, precision=HIGHEST, preferred_element_type=F32)
    gt_t = gt.T
    b_t = bcols.T
    ones_ext = jnp.ones((L, LANES), BF16)

    for h in range(M_HEADS):
        icol = gt[:, h:h + 1]
        bcol = bcols[:, M_HEADS + h:M_HEADS + h + 1]
        irow = gt_t[h:h + 1, :]
        brow = b_t[M_HEADS + h:M_HEADS + h + 1, :]
        m_prev = m_scr[h, 0:1, 0:1]
        b_last = bcol[L - 1:L, :]

        dmat = jnp.where(tri, bcol - brow + irow, -jnp.inf)
        inter = bcol + m_prev
        m_t = jnp.maximum(inter, jnp.max(dmat, axis=-1, keepdims=True))
        dexp = jnp.exp(dmat - m_t)
        inter_w = jnp.exp(inter - m_t)

        q = (qk[:, h * M_QK_DIM:(h + 1) * M_QK_DIM] * (M_QK_DIM ** -0.5)).astype(BF16)
        kf = qk[:, M_QK + h * M_QK_DIM:M_QK + (h + 1) * M_QK_DIM]
        v_ext = jnp.concatenate([v_ref[:, h * M_V_DIM:(h + 1) * M_V_DIM], ones_ext], axis=1)

        scores = (_dot_nt(q, kf.astype(BF16)) * dexp).astype(BF16)
        c_old = c_scr[h]
        nd = _dot(scores, v_ext) + inter_w * _dot(q, c_old.astype(BF16))
        den = nd[:, M_V_DIM:]
        inv = 1.0 / jnp.maximum(jnp.abs(den), jnp.exp(-m_t))
        hs = [nd[:, j * LANES:(j + 1) * LANES] * inv for j in range(M_V_DIM // LANES)]
        hh = jnp.concatenate(hs, axis=1)

        m_new = m_t[L - 1:L, :]
        w_s = jnp.exp(b_last - bcol + icol - m_new)
        decay = jnp.exp(b_last + m_prev - m_new)
        kw = (kf * w_s).astype(BF16)
        c_scr[h] = decay * c_old + _dot_tn(kw, v_ext)
        m_scr[h] = jnp.broadcast_to(m_new, (SUBLANES, LANES))

        hn = hh * lax.rsqrt(jnp.mean(hh * hh, axis=-1, keepdims=True) + EPS)
        sl = slice(h * M_V_DIM, (h + 1) * M_V_DIM)
        og = _sigmoid(o_ref[:, sl].astype(F32))
        y_ref[:, sl] = (og * hn * hg_ref[:, sl]).astype(BF16)


def _mlstm(proj, gates, conv_w, conv_b, head_g, nb, nc):
    t = proj.shape[0]
    L = M_CHUNK
    row = lambda b, c: b * nc + c
    return pl.pallas_call(
        _mlstm_kernel,
        out_shape=jax.ShapeDtypeStruct((t, M_V), BF16),
        grid=(nb, nc),
        in_specs=[
            pl.BlockSpec((L, 2 * M_QK), lambda b, c: (row(b, c), 0)),
            pl.BlockSpec((L, M_V), lambda b, c: (row(b, c), 1)),
            pl.BlockSpec((L, M_V), lambda b, c: (row(b, c), 2)),
            pl.BlockSpec((L, LANES), lambda b, c: (row(b, c), 0)),
            pl.BlockSpec((CONV_W, 2 * M_QK), lambda b, c: (0, 0)),
            pl.BlockSpec((1, 2 * M_QK), lambda b, c: (0, 0)),
            pl.BlockSpec((1, M_V), lambda b, c: (0, 0)),
        ],
        out_specs=pl.BlockSpec((L, M_V), lambda b, c: (row(b, c), 0)),
        scratch_shapes=[
            pltpu.VMEM((SUBLANES + L, 2 * M_QK), F32),
            pltpu.VMEM((M_HEADS, M_QK_DIM, M_V_DIM + LANES), F32),
            pltpu.VMEM((M_HEADS, SUBLANES, LANES), F32),
        ],
        compiler_params=_params("parallel", "arbitrary"),
        name="mlstm",
    )(proj, proj, proj, gates, conv_w, conv_b, head_g)


def _hgrn_kernel(gf_ref, gi_ref, gq_ref, gg_ref, lb_ref, hg_ref, y_ref, st_scr):
    L, SUB, NSUB = G_CHUNK, G_SUB, G_CHUNK // G_SUB

    @pl.when(pl.program_id(1) == 0)
    def _():
        st_scr[...] = jnp.zeros(st_scr.shape, F32)

    lb = lb_ref[...]
    f = lb + (1.0 - lb) * _sigmoid(gf_ref[...].astype(F32))
    k_all = 1.0 - f
    cum_all = jnp.dot(_tril(L).astype(F32), jnp.log(f), precision=HIGHEST,
                      preferred_element_type=F32)
    gq = gq_ref[...].astype(F32)
    q_all = gq * _sigmoid(gq)

    row = lax.broadcasted_iota(jnp.int32, (L, G_DIM), 0)
    row_sub = lax.broadcasted_iota(jnp.int32, (SUB, G_DIM), 0)
    ones_sq = jnp.ones((G_DIM, G_DIM), BF16)

    for h in range(G_HEADS):
        sl = slice(h * G_DIM, (h + 1) * G_DIM)
        cum, q, k = cum_all[:, sl], q_all[:, sl], k_all[:, sl]
        iv = gi_ref[:, sl]
        ivf = iv.astype(F32)
        ends = [cum[SUB * b + SUB - 1:SUB * b + SUB, :] for b in range(NSUB)]
        last = ends[NSUB - 1]
        end_of_row = jnp.concatenate([jnp.broadcast_to(e, (SUB, G_DIM)) for e in ends], axis=0)

        khat = k * jnp.exp(end_of_row - cum)
        qs, ks = [], []
        for b in range(NSUB - 1):
            lo = SUB * (b + 1)
            qs.append(jnp.where(row >= lo, q * jnp.exp(jnp.minimum(cum - ends[b], 0.0)), 0.0))
            ks.append(jnp.where((row >= lo - SUB) & (row < lo), khat, 0.0))
        att = _dot_nt(jnp.concatenate(qs, axis=1).astype(BF16),
                      jnp.concatenate(ks, axis=1).astype(BF16))
        o = _dot(att.astype(BF16), iv)

        st = st_scr[h]
        o = o + _dot_nt((q * jnp.exp(cum)).astype(BF16), st.astype(BF16))

        o_diag = []
        for a in range(NSUB):
            r0 = SUB * a
            q_a, k_a, c_a, i_a = (z[r0:r0 + SUB, :] for z in (q, k, cum, ivf))
            ms = []
            for s in range(SUB):
                e = jnp.exp(jnp.where(row_sub >= s, c_a - c_a[s:s + 1, :], -jnp.inf))
                ms.append(q_a * (k_a[s:s + 1, :] * e))
            red = _dot(jnp.concatenate(ms, axis=0).astype(BF16), ones_sq)
            acc = red[0:SUB, :] * i_a[0:1, :]
            for s in range(1, SUB):
                acc = acc + red[s * SUB:(s + 1) * SUB, :] * i_a[s:s + 1, :]
            o_diag.append(acc)
        o = o + jnp.concatenate(o_diag, axis=0)

        kd = (k * jnp.exp(last - cum)).astype(BF16)
        st_scr[h] = jnp.exp(last) * st + _dot_tn(iv, kd)

        hn = o * lax.rsqrt(jnp.mean(o * o, axis=-1, keepdims=True) + EPS)
        gg = gg_ref[:, sl].astype(F32)
        y_ref[:, sl] = (hn * hg_ref[:, sl] * (gg * _sigmoid(gg))).astype(BF16)


def _hgrn(proj, lb, head_g, nb, nc, col0):
    t = proj.shape[0]
    L = G_CHUNK
    row = lambda b, c: b * nc + c
    spec = lambda j: pl.BlockSpec((L, G_W), lambda b, c: (row(b, c), col0 + j))
    return pl.pallas_call(
        _hgrn_kernel,
        out_shape=jax.ShapeDtypeStruct((t, G_W), BF16),
        grid=(nb, nc),
        in_specs=[spec(0), spec(1), spec(2), spec(3),
                  pl.BlockSpec((1, G_W), lambda b, c: (0, 0)),
                  pl.BlockSpec((1, G_W), lambda b, c: (0, 0))],
        out_specs=pl.BlockSpec((L, G_W), lambda b, c: (row(b, c), 0)),
        scratch_shapes=[pltpu.VMEM((G_HEADS, G_DIM, G_DIM), F32)],
        compiler_params=_params("parallel", "arbitrary"),
        name="hgrn2",
    )(proj, proj, proj, proj, lb, head_g)


def _merge_kernel(x_ref, ym_ref, yg_ref, am_ref, ag_ref, wm_ref, wg_ref, wo_ref, g2_ref,
                  x1_ref, u2_ref):
    zm = _dot(ym_ref[...], wm_ref[...])
    zg = _dot(yg_ref[...], wg_ref[...])
    z = _sigmoid(am_ref[...].astype(F32)) * zm + _sigmoid(ag_ref[...].astype(F32)) * zg
    x1 = x_ref[...] + _dot(z.astype(BF16), wo_ref[...])
    x1_ref[...] = x1
    u2 = x1 * lax.rsqrt(jnp.mean(x1 * x1, axis=-1, keepdims=True) + EPS) * g2_ref[...]
    u2_ref[...] = u2.astype(BF16)


def _merge(x2, ym, yg, proj, wm, wg, wo, g2, tm, col_am):
    t, d = x2.shape
    rows = lambda w: pl.BlockSpec((tm, w), lambda i: (i, 0))
    full = lambda a: pl.BlockSpec(a.shape, lambda i: (0, 0))
    return pl.pallas_call(
        _merge_kernel,
        out_shape=(jax.ShapeDtypeStruct((t, d), F32), jax.ShapeDtypeStruct((t, d), BF16)),
        grid=(t // tm,),
        in_specs=[rows(d), rows(M_V), rows(G_W),
                  pl.BlockSpec((tm, d), lambda i: (i, col_am)),
                  pl.BlockSpec((tm, d), lambda i: (i, col_am + 1)),
                  full(wm), full(wg), full(wo), full(g2)],
        out_specs=(rows(d), rows(d)),
        compiler_params=_params("parallel"),
        name="merge",
    )(x2, ym, yg, proj, proj, wm, wg, wo, g2)


def _top_rows(vals, k):
    out = []
    for _ in range(k):
        m = jnp.max(vals, axis=0, keepdims=True)
        out.append(m)
        vals = jnp.where(vals == m, -jnp.inf, vals)
    return out


def _route_kernel(u2_ref, wq_ref, keys_ref, thr_ref, w0_ref, s1_ref, e1_ref, cand_scr):
    K = P_TOPK
    tb = u2_ref.shape[0]
    q_t = _dot_nt(wq_ref[...], u2_ref[...])
    pairs = [(r, c) for r in range(K) for c in range(K) if (r + 1) * (c + 1) <= K]
    n_pad = cand_scr.shape[0] - len(pairs)
    cand_scr[len(pairs):, :] = jnp.full((n_pad, tb), -jnp.inf, F32)

    for h in range(P_HEADS):
        s = []
        for p in range(2):
            r0 = (2 * h + p) * P_NKEYS
            s.append(_dot(keys_ref[2 * h + p], q_t[r0:r0 + P_NKEYS, :].astype(BF16)))
        a = _top_rows(s[0], K)
        b = _top_rows(s[1], K)
        for n, (r, c) in enumerate(pairs):
            cand_scr[n:n + 1, :] = a[r] + b[c]
        top = _top_rows(cand_scr[...], K)
        z = jnp.ones_like(top[0])
        for r in range(1, K):
            z = z + jnp.exp(top[r] - top[0])
        thr_ref[h] = top[K - 1] - s[0]
        w0_ref[h] = 0.5 * jnp.exp(s[0] - a[0]) / z
        s1_ref[h] = s[1]
        e1_ref[h] = jnp.exp(s[1] - b[0])


def _route(u2, wq_t, keys, tb):
    t, d = u2.shape
    n_pairs = sum(1 for r in range(P_TOPK) for c in range(P_TOPK) if (r + 1) * (c + 1) <= P_TOPK)
    n_cand = -(-n_pairs // SUBLANES) * SUBLANES
    out = jax.ShapeDtypeStruct((P_HEADS, P_NKEYS, t), F32)
    ospec = pl.BlockSpec((P_HEADS, P_NKEYS, tb), lambda i: (0, 0, i))
    return pl.pallas_call(
        _route_kernel,
        out_shape=(out, out, out, out),
        grid=(t // tb,),
        in_specs=[pl.BlockSpec((tb, d), lambda i: (i, 0)),
                  pl.BlockSpec(wq_t.shape, lambda i: (0, 0)),
                  pl.BlockSpec(keys.shape, lambda i: (0, 0, 0))],
        out_specs=(ospec, ospec, ospec, ospec),
        scratch_shapes=[pltpu.VMEM((n_cand, tb), F32)],
        compiler_params=_params("parallel"),
        name="route",
    )(u2, wq_t, keys)


PEER_JB = 32
PEER_IB = 4


def _peer_step(u2_ref, u_ref, vt_ref, thr_ref, w0_ref, s1_ref, e1_ref, acc_scr,
               h_in, h_out, p_in, p_out):
    eb, d = u_ref.shape
    tb = u2_ref.shape[0]
    n_grp = eb // (PEER_IB * P_NKEYS)
    for ib in range(n_grp):
        er = slice(ib * (eb // n_grp), (ib + 1) * (eb // n_grp))
        dr = slice(ib * (d // n_grp), (ib + 1) * (d // n_grp))
        h_out[er, :] = _dot_nt(u_ref[er, :], u2_ref[...])
        acc_scr[dr, :] += _dot(vt_ref[dr, :], p_in[...])
        for tc in range(tb // LANES):
            lanes = slice(tc * LANES, (tc + 1) * LANES)
            for jb in range(P_NKEYS // PEER_JB):
                jrows = slice(jb * PEER_JB, (jb + 1) * PEER_JB)
                gate = [None] * PEER_IB
                for h in range(P_HEADS):
                    s1 = s1_ref[h, jrows, lanes]
                    e1 = e1_ref[h, jrows, lanes]
                    for k in range(PEER_IB):
                        i = ib * PEER_IB + k
                        term = (jnp.where(s1 >= thr_ref[h, i:i + 1, lanes], e1, 0.0)
                                * w0_ref[h, i:i + 1, lanes])
                        gate[k] = term if gate[k] is None else gate[k] + term
                for k in range(PEER_IB):
                    r0 = (ib * PEER_IB + k) * P_NKEYS + jb * PEER_JB
                    hv = h_in[r0:r0 + PEER_JB, lanes]
                    act = hv * (1.0 + lax.erf(hv * (2.0 ** -0.5)))
                    p_out[r0:r0 + PEER_JB, lanes] = (act * gate[k]).astype(BF16)


def _peer_kernel(n_e, u2_ref, x1_ref, fg_ref, u_ref, vt_ref, thr_ref, w0_ref, s1_ref, e1_ref,
                 out_ref, acc_scr, h_scr, p_scr):
    f = pl.program_id(0)

    @pl.when(f == 0)
    def _():
        acc_scr[...] = jnp.zeros(acc_scr.shape, F32)
        h_scr[1] = jnp.zeros(h_scr.shape[1:], F32)
        p_scr[1] = jnp.zeros(p_scr.shape[1:], BF16)

    step = functools.partial(_peer_step, u2_ref, u_ref, vt_ref, thr_ref, w0_ref, s1_ref, e1_ref,
                             acc_scr)

    @pl.when(f % 2 == 0)
    def _():
        step(h_scr.at[1], h_scr.at[0], p_scr.at[1], p_scr.at[0])

    @pl.when(f % 2 == 1)
    def _():
        step(h_scr.at[0], h_scr.at[1], p_scr.at[0], p_scr.at[1])

    @pl.when((f >= 2) & ((f - 2) % n_e == n_e - 1))
    def _():
        x2 = x1_ref[...] + acc_scr[...].T
        out_ref[...] = (x2 * lax.rsqrt(jnp.mean(x2 * x2, axis=-1, keepdims=True) + EPS)
                        * fg_ref[...])
        acc_scr[...] = jnp.zeros(acc_scr.shape, F32)


def _peer(u2, x1, fg, u_emb, v_t, thr, w0, s1, e1, tb, eb):
    t, d = u2.shape
    n_e = u_emb.shape[0] // eb
    n_i = eb // P_NKEYS
    n = (t // tb) * n_e
    blk = lambda f, lag: jnp.clip(f - lag, 0, n - 1)
    tok = lambda lag: (lambda f: (blk(f, lag) // n_e, 0))
    gate_blk = lambda f: (0, blk(f, 1) % n_e, blk(f, 1) // n_e)
    key_blk = lambda f: (0, 0, blk(f, 1) // n_e)
    return pl.pallas_call(
        functools.partial(_peer_kernel, n_e),
        out_shape=jax.ShapeDtypeStruct((t, d), F32),
        grid=(n + 2,),
        in_specs=[pl.BlockSpec((tb, d), tok(0)),
                  pl.BlockSpec((tb, d), tok(2)),
                  pl.BlockSpec((1, d), lambda f: (0, 0)),
                  pl.BlockSpec((eb, d), lambda f: (blk(f, 0) % n_e, 0)),
                  pl.BlockSpec((d, eb), lambda f: (0, blk(f, 2) % n_e)),
                  pl.BlockSpec((P_HEADS, n_i, tb), gate_blk),
                  pl.BlockSpec((P_HEADS, n_i, tb), gate_blk),
                  pl.BlockSpec((P_HEADS, P_NKEYS, tb), key_blk),
                  pl.BlockSpec((P_HEADS, P_NKEYS, tb), key_blk)],
        out_specs=pl.BlockSpec((tb, d), tok(2)),
        scratch_shapes=[pltpu.VMEM((d, tb), F32), pltpu.VMEM((2, eb, tb), F32),
                        pltpu.VMEM((2, eb, tb), BF16)],
        compiler_params=_params("arbitrary"),
        name="peer",
    )(u2, x1, fg, u_emb, v_t, thr, w0, s1, e1)


def _tiles(t):
    pick = lambda pref: next(c for c in (pref, 512, 256, 128) if c <= pref and t % c == 0)
    return dict(inproj_tm=pick(1024), inproj_tn=1024, merge_tm=pick(512),
                route_tb=pick(256), peer_tb=pick(512), peer_eb=1024)


def kernel(x, norm1_g, w_in, b_in, conv_w, conv_b, m_head_g, lb_table, g_head_g, w_br_m, w_br_g,
           w_out, norm2_g, w_pq, sub_keys, u_emb, v_emb, final_g):
    nb, seq, d = x.shape
    t = nb * seq
    depth = w_in.shape[0]
    assert d == 1024 and seq % M_CHUNK == 0 and t % LANES == 0
    tl = _tiles(t)
    n_gate = 2 * M_HEADS
    g0 = 2 * M_QK + 2 * M_V
    assert (w_in.shape[2] - n_gate) % tl["inproj_tn"] == 0

    lb_all = jnp.cumsum(jax.nn.softmax(lb_table.astype(F32), axis=0), axis=0)
    x2 = x.reshape(t, d)
    row = lambda a: a.reshape(1, -1).astype(F32)

    for l in range(depth):
        w_main = jnp.concatenate([w_in[l][:, :g0], w_in[l][:, g0 + n_gate:]], axis=1).astype(BF16)
        b_main = jnp.concatenate([b_in[l][:g0], b_in[l][g0 + n_gate:]]).reshape(1, -1)
        w_gate = jnp.pad(w_in[l][:, g0:g0 + n_gate], ((0, 0), (0, LANES - n_gate)))
        b_gate = jnp.pad(b_in[l][g0:g0 + n_gate], (0, LANES - n_gate)).reshape(1, -1)

        proj, gates = _inproj(x2, row(norm1_g[l]), w_main, b_main, w_gate, b_gate,
                              tl["inproj_tm"], tl["inproj_tn"])
        y_m = _mlstm(proj, gates, conv_w[l], row(conv_b[l]), row(m_head_g[l]),
                     nb, seq // M_CHUNK)
        y_g = _hgrn(proj, row(lb_all[l]), row(g_head_g[l]), nb, seq // G_CHUNK, g0 // G_W)
        x1, u2 = _merge(x2, y_m, y_g, proj, w_br_m[l].astype(BF16), w_br_g[l].astype(BF16),
                        w_out[l].astype(BF16), row(norm2_g[l]), tl["merge_tm"],
                        (g0 + 4 * G_W) // d)

        keys = sub_keys[l].reshape(2 * P_HEADS, P_NKEYS, -1).astype(BF16)
        thr, w0, s1, e1 = _route(u2, w_pq[l].T.astype(BF16), keys, tl["route_tb"])
        assert depth == 1
        x2 = _peer(u2, x1, row(final_g), u_emb[l].astype(BF16), v_emb[l].T.astype(BF16),
                   thr, w0, s1, e1, tl["peer_tb"], tl["peer_eb"])
    return x2.reshape(nb, seq, d)
```

```python
import functools

import jax
import jax.numpy as jnp
from jax import lax
from jax.experimental import pallas as pl
from jax.experimental.pallas import tpu as pltpu

F32 = jnp.float32
BF16 = jnp.bfloat16
HIGHEST = lax.Precision.HIGHEST

EPS = 1e-6
LANES = 128
SUBLANES = 8
VMEM_LIMIT = 48 * 1024 * 1024

M_HEADS, M_QK_DIM, M_V_DIM, M_CHUNK, CONV_W = 4, 256, 512, 128, 4
G_HEADS, G_DIM, G_CHUNK, G_SUB = 8, 128, 64, 16
P_HEADS, P_NKEYS, P_TOPK = 8, 128, 16
M_QK = M_HEADS * M_QK_DIM
M_V = M_HEADS * M_V_DIM
G_W = G_HEADS * G_DIM


def _params(*sem):
    return pltpu.CompilerParams(dimension_semantics=sem, vmem_limit_bytes=VMEM_LIMIT)


def _dot(a, b):
    return jnp.dot(a, b, preferred_element_type=F32)


def _dot_nt(a, b):
    return lax.dot_general(a, b, (((1,), (1,)), ((), ())), preferred_element_type=F32)


def _dot_tn(a, b):
    return lax.dot_general(a, b, (((0,), (0,)), ((), ())), preferred_element_type=F32)


def _sigmoid(x):
    return 1.0 / (1.0 + jnp.exp(-x))


def _tril(n):
    r = lax.broadcasted_iota(jnp.int32, (n, n), 0)
    c = lax.broadcasted_iota(jnp.int32, (n, n), 1)
    return (r >= c)


def _inproj_kernel(x_ref, g_ref, w_ref, b_ref, wg_ref, bg_ref, out_ref, gate_ref, u_scr):
    @pl.when(pl.program_id(1) == 0)
    def _():
        x = x_ref[...]
        u = x * lax.rsqrt(jnp.mean(x * x, axis=-1, keepdims=True) + EPS) * g_ref[...]
        u_scr[...] = u.astype(BF16)
        gate_ref[...] = jnp.dot(u, wg_ref[...], precision=HIGHEST,
                                preferred_element_type=F32) + bg_ref[...]

    out_ref[...] = (_dot(u_scr[...], w_ref[...]) + b_ref[...]).astype(BF16)


def _inproj(x2, g, w, b, wg, bg, tm, tn):
    t, d = x2.shape
    n = w.shape[1]
    return pl.pallas_call(
        _inproj_kernel,
        out_shape=(jax.ShapeDtypeStruct((t, n), BF16), jax.ShapeDtypeStruct((t, LANES), F32)),
        grid=(t // tm, n // tn),
        in_specs=[
            pl.BlockSpec((tm, d), lambda i, j: (i, 0)),
            pl.BlockSpec((1, d), lambda i, j: (0, 0)),
            pl.BlockSpec((d, tn), lambda i, j: (0, j)),
            pl.BlockSpec((1, tn), lambda i, j: (0, j)),
            pl.BlockSpec((d, LANES), lambda i, j: (0, 0)),
            pl.BlockSpec((1, LANES), lambda i, j: (0, 0)),
        ],
        out_specs=(pl.BlockSpec((tm, tn), lambda i, j: (i, j)),
                   pl.BlockSpec((tm, LANES), lambda i, j: (i, 0))),
        scratch_shapes=[pltpu.VMEM((tm, d), BF16)],
        compiler_params=_params("parallel", "arbitrary"),
        name="inproj",
    )(x2, g, w, b, wg, bg)


def _mlstm_kernel(qk_ref, v_ref, o_ref, gt_ref, cw_ref, cb_ref, hg_ref, y_ref,
                  ext_scr, c_scr, m_scr):
    L = M_CHUNK

    @pl.when(pl.program_id(1) == 0)
    def _():
        ext_scr[0:SUBLANES, :] = jnp.zeros((SUBLANES, 2 * M_QK), F32)
        c_scr[...] = jnp.zeros(c_scr.shape, F32)
        m_scr[...] = jnp.zeros(m_scr.shape, F32)

    cur = qk_ref[...].astype(F32)
    ext_scr[SUBLANES:SUBLANES + L, :] = cur
    acc = cb_ref[...] + cw_ref[CONV_W - 1:CONV_W, :] * cur
    for j in range(CONV_W - 1):
        off = SUBLANES - (CONV_W - 1) + j
        acc = acc + cw_ref[j:j + 1, :] * ext_scr[off:off + L, :]
    ext_scr[0:SUBLANES, :] = cur[L - SUBLANES:L, :]
    qk = acc * _sigmoid(acc)

    gt = gt_ref[...]
    lf = jnp.minimum(gt, 0.0) - jnp.log(1.0 + jnp.exp(-jnp.abs(gt)))
    tri = _tril(L)
    bcols = jnp.dot(tri.astype(F32), lf, precision=HIGHEST, preferred_element_type=F32)
    gt_t = gt.T
    b_t = bcols.T
    ones_ext = jnp.ones((L, LANES), BF16)

    for h in range(M_HEADS):
        icol = gt[:, h:h + 1]
        bcol = bcols[:, M_HEADS + h:M_HEADS + h + 1]
        irow = gt_t[h:h + 1, :]
        brow = b_t[M_HEADS + h:M_HEADS + h + 1, :]
        m_prev = m_scr[h, 0:1, 0:1]
        b_last = bcol[L - 1:L, :]

        dmat = jnp.where(tri, bcol - brow + irow, -jnp.inf)
        inter = bcol + m_prev
        m_t = jnp.maximum(inter, jnp.max(dmat, axis=-1, keepdims=True))
        dexp = jnp.exp(dmat - m_t)
        inter_w = jnp.exp(inter - m_t)

        q = (qk[:, h * M_QK_DIM:(h + 1) * M_QK_DIM] * (M_QK_DIM ** -0.5)).astype(BF16)
        kf = qk[:, M_QK + h * M_QK_DIM:M_QK + (h + 1) * M_QK_DIM]
        v_ext = jnp.concatenate([v_ref[:, h * M_V_DIM:(h + 1) * M_V_DIM], ones_ext], axis=1)

        scores = (_dot_nt(q, kf.astype(BF16)) * dexp).astype(BF16)
        c_old = c_scr[h]
        nd = _dot(scores, v_ext) + inter_w * _dot(q, c_old.astype(BF16))
        den = nd[:, M_V_DIM:]
        inv = 1.0 / jnp.maximum(jnp.abs(den), jnp.exp(-m_t))
        hs = [nd[:, j * LANES:(j + 1) * LANES] * inv for j in range(M_V_DIM // LANES)]
        hh = jnp.concatenate(hs, axis=1)

        m_new = m_t[L - 1:L, :]
        w_s = jnp.exp(b_last - bcol + icol - m_new)
        decay = jnp.exp(b_last + m_prev - m_new)
        kw = (kf * w_s).astype(BF16)
        c_scr[h] = decay * c_old + _dot_tn(kw, v_ext)
        m_scr[h] = jnp.broadcast_to(m_new, (SUBLANES, LANES))

        hn = hh * lax.rsqrt(jnp.mean(hh * hh, axis=-1, keepdims=True) + EPS)
        sl = slice(h * M_V_DIM, (h + 1) * M_V_DIM)
        og = _sigmoid(o_ref[:, sl].astype(F32))
        y_ref[:, sl] = (og * hn * hg_ref[:, sl]).astype(BF16)


def _mlstm(proj, gates, conv_w, conv_b, head_g, nb, nc):
    t = proj.shape[0]
    L = M_CHUNK
    row = lambda b, c: b * nc + c
    return pl.pallas_call(
        _mlstm_kernel,
        out_shape=jax.ShapeDtypeStruct((t, M_V), BF16),
        grid=(nb, nc),
        in_specs=[
            pl.BlockSpec((L, 2 * M_QK), lambda b, c: (row(b, c), 0)),
            pl.BlockSpec((L, M_V), lambda b, c: (row(b, c), 1)),
            pl.BlockSpec((L, M_V), lambda b, c: (row(b, c), 2)),
            pl.BlockSpec((L, LANES), lambda b, c: (row(b, c), 0)),
            pl.BlockSpec((CONV_W, 2 * M_QK), lambda b, c: (0, 0)),
            pl.BlockSpec((1, 2 * M_QK), lambda b, c: (0, 0)),
            pl.BlockSpec((1, M_V), lambda b, c: (0, 0)),
        ],
        out_specs=pl.BlockSpec((L, M_V), lambda b, c: (row(b, c), 0)),
        scratch_shapes=[
            pltpu.VMEM((SUBLANES + L, 2 * M_QK), F32),
            pltpu.VMEM((M_HEADS, M_QK_DIM, M_V_DIM + LANES), F32),
            pltpu.VMEM((M_HEADS, SUBLANES, LANES), F32),
        ],
        compiler_params=_params("parallel", "arbitrary"),
        name="mlstm",
    )(proj, proj, proj, gates, conv_w, conv_b, head_g)


def _hgrn_kernel(gf_ref, gi_ref, gq_ref, gg_ref, lb_ref, hg_ref, y_ref, st_scr):
    L, SUB, NSUB = G_CHUNK, G_SUB, G_CHUNK // G_SUB

    @pl.when(pl.program_id(1) == 0)
    def _():
        st_scr[...] = jnp.zeros(st_scr.shape, F32)

    lb = lb_ref[...]
    f = lb + (1.0 - lb) * _sigmoid(gf_ref[...].astype(F32))
    k_all = 1.0 - f
    cum_all = jnp.dot(_tril(L).astype(F32), jnp.log(f), precision=HIGHEST,
                      preferred_element_type=F32)
    gq = gq_ref[...].astype(F32)
    q_all = gq * _sigmoid(gq)

    row = lax.broadcasted_iota(jnp.int32, (L, G_DIM), 0)
    row_sub = lax.broadcasted_iota(jnp.int32, (SUB, G_DIM), 0)
    ones_sq = jnp.ones((G_DIM, G_DIM), BF16)

    for h in range(G_HEADS):
        sl = slice(h * G_DIM, (h + 1) * G_DIM)
        cum, q, k = cum_all[:, sl], q_all[:, sl], k_all[:, sl]
        iv = gi_ref[:, sl]
        ivf = iv.astype(F32)
        ends = [cum[SUB * b + SUB - 1:SUB * b + SUB, :] for b in range(NSUB)]
        last = ends[NSUB - 1]
        end_of_row = jnp.concatenate([jnp.broadcast_to(e, (SUB, G_DIM)) for e in ends], axis=0)

        khat = k * jnp.exp(end_of_row - cum)
        qs, ks = [], []
        for b in range(NSUB - 1):
            lo = SUB * (b + 1)
            qs.append(jnp.where(row >= lo, q * jnp.exp(jnp.minimum(cum - ends[b], 0.0)), 0.0))
            ks.append(jnp.where((row >= lo - SUB) & (row < lo), khat, 0.0))
        att = _dot_nt(jnp.concatenate(qs, axis=1).astype(BF16),
                      jnp.concatenate(ks, axis=1).astype(BF16))
        o = _dot(att.astype(BF16), iv)

        st = st_scr[h]
        o = o + _dot_nt((q * jnp.exp(cum)).astype(BF16), st.astype(BF16))

        o_diag = []
        for a in range(NSUB):
            r0 = SUB * a
            q_a, k_a, c_a, i_a = (z[r0:r0 + SUB, :] for z in (q, k, cum, ivf))
            ms = []
            for s in range(SUB):
                e = jnp.exp(jnp.where(row_sub >= s, c_a - c_a[s:s + 1, :], -jnp.inf))
                ms.append(q_a * (k_a[s:s + 1, :] * e))
            red = _dot(jnp.concatenate(ms, axis=0).astype(BF16), ones_sq)
            acc = red[0:SUB, :] * i_a[0:1, :]
            for s in range(1, SUB):
                acc = acc + red[s * SUB:(s + 1) * SUB, :] * i_a[s:s + 1, :]
            o_diag.append(acc)
        o = o + jnp.concatenate(o_diag, axis=0)

        kd = (k * jnp.exp(last - cum)).astype(BF16)
        st_scr[h] = jnp.exp(last) * st + _dot_tn(iv, kd)

        hn = o * lax.rsqrt(jnp.mean(o * o, axis=-1, keepdims=True) + EPS)
        gg = gg_ref[:, sl].astype(F32)
        y_ref[:, sl] = (hn * hg_ref[:, sl] * (gg * _sigmoid(gg))).astype(BF16)


def _hgrn(proj, lb, head_g, nb, nc, col0):
    t = proj.shape[0]
    L = G_CHUNK
    row = lambda b, c: b * nc + c
    spec = lambda j: pl.BlockSpec((L, G_W), lambda b, c: (row(b, c), col0 + j))
    return pl.pallas_call(
        _hgrn_kernel,
        out_shape=jax.ShapeDtypeStruct((t, G_W), BF16),
        grid=(nb, nc),
        in_specs=[spec(0), spec(1), spec(2), spec(3),
                  pl.BlockSpec((1, G_W), lambda b, c: (0, 0)),
                  pl.BlockSpec((1, G_W), lambda b, c: (0, 0))],
        out_specs=pl.BlockSpec((L, G_W), lambda b, c: (row(b, c), 0)),
        scratch_shapes=[pltpu.VMEM((G_HEADS, G_DIM, G_DIM), F32)],
        compiler_params=_params("parallel", "arbitrary"),
        name="hgrn2",
    )(proj, proj, proj, proj, lb, head_g)


def _merge_kernel(x_ref, ym_ref, yg_ref, am_ref, ag_ref, wm_ref, wg_ref, wo_ref, g2_ref,
                  x1_ref, u2_ref):
    zm = _dot(ym_ref[...], wm_ref[...])
    zg = _dot(yg_ref[...], wg_ref[...])
    z = _sigmoid(am_ref[...].astype(F32)) * zm + _sigmoid(ag_ref[...].astype(F32)) * zg
    x1 = x_ref[...] + _dot(z.astype(BF16), wo_ref[...])
    x1_ref[...] = x1
    u2 = x1 * lax.rsqrt(jnp.mean(x1 * x1, axis=-1, keepdims=True) + EPS) * g2_ref[...]
    u2_ref[...] = u2.astype(BF16)


def _merge(x2, ym, yg, proj, wm, wg, wo, g2, tm, col_am):
    t, d = x2.shape
    rows = lambda w: pl.BlockSpec((tm, w), lambda i: (i, 0))
    full = lambda a: pl.BlockSpec(a.shape, lambda i: (0, 0))
    return pl.pallas_call(
        _merge_kernel,
        out_shape=(jax.ShapeDtypeStruct((t, d), F32), jax.ShapeDtypeStruct((t, d), BF16)),
        grid=(t // tm,),
        in_specs=[rows(d), rows(M_V), rows(G_W),
                  pl.BlockSpec((tm, d), lambda i: (i, col_am)),
                  pl.BlockSpec((tm, d), lambda i: (i, col_am + 1)),
                  full(wm), full(wg), full(wo), full(g2)],
        out_specs=(rows(d), rows(d)),
        compiler_params=_params("parallel"),
        name="merge",
    )(x2, ym, yg, proj, proj, wm, wg, wo, g2)


def _top_rows(vals, k, want_rank=False):
    out = []
    rank = jnp.full(vals.shape, float(k), F32) if want_rank else None
    for r in range(k):
        m = jnp.max(vals, axis=0, keepdims=True)
        out.append(m)
        hit = vals == m
        if want_rank:
            rank = jnp.where(hit, float(r), rank)
        vals = jnp.where(hit, -jnp.inf, vals)
    return out, rank


def _route_kernel(u2_ref, wq_ref, keys_ref, cnt_ref, w0_ref, r1_ref, e1_ref, cand_scr):
    K = P_TOPK
    tb = u2_ref.shape[0]
    q_t = _dot_nt(wq_ref[...], u2_ref[...])
    pairs = [(r, c) for r in range(K) for c in range(K) if (r + 1) * (c + 1) <= K]
    n_pad = cand_scr.shape[0] - len(pairs)
    cand_scr[len(pairs):, :] = jnp.full((n_pad, tb), -jnp.inf, F32)

    for h in range(P_HEADS):
        s = []
        for p in range(2):
            r0 = (2 * h + p) * P_NKEYS
            s.append(_dot(keys_ref[2 * h + p], q_t[r0:r0 + P_NKEYS, :].astype(BF16)))
        a, _ = _top_rows(s[0], K)
        b, rank1 = _top_rows(s[1], K, want_rank=True)
        for n, (r, c) in enumerate(pairs):
            cand_scr[n:n + 1, :] = a[r] + b[c]
        top, _ = _top_rows(cand_scr[...], K)
        z = jnp.ones_like(top[0])
        for r in range(1, K):
            z = z + jnp.exp(top[r] - top[0])
        thr = top[K - 1] - s[0]
        cnt = jnp.zeros_like(thr)
        for c in range(K):
            cnt = cnt + jnp.where(b[c] >= thr, 1.0, 0.0)
        cnt_ref[h] = cnt
        w0_ref[h] = 0.5 * jnp.exp(s[0] - a[0]) / z
        r1_ref[h] = rank1.astype(BF16)
        e1_ref[h] = jnp.exp(s[1] - b[0]).astype(BF16)


def _route(u2, wq_t, keys, tb):
    t, d = u2.shape
    n_pairs = sum(1 for r in range(P_TOPK) for c in range(P_TOPK) if (r + 1) * (c + 1) <= P_TOPK)
    n_cand = -(-n_pairs // SUBLANES) * SUBLANES
    out = lambda dt: jax.ShapeDtypeStruct((P_HEADS, P_NKEYS, t), dt)
    ospec = pl.BlockSpec((P_HEADS, P_NKEYS, tb), lambda i: (0, 0, i))
    return pl.pallas_call(
        _route_kernel,
        out_shape=(out(F32), out(F32), out(BF16), out(BF16)),
        grid=(t // tb,),
        in_specs=[pl.BlockSpec((tb, d), lambda i: (i, 0)),
                  pl.BlockSpec(wq_t.shape, lambda i: (0, 0)),
                  pl.BlockSpec(keys.shape, lambda i: (0, 0, 0))],
        out_specs=(ospec, ospec, ospec, ospec),
        scratch_shapes=[pltpu.VMEM((n_cand, tb), F32)],
        compiler_params=_params("parallel"),
        name="route",
    )(u2, wq_t, keys)


PEER_JB = 64
PEER_IB = 4
BF16_ROWS = 2 * SUBLANES


def _bcast_bf16(row, rows):
    one = jnp.broadcast_to(row, (BF16_ROWS, row.shape[1])).astype(BF16)
    return jnp.concatenate([one] * (rows // BF16_ROWS), axis=0)


def _peer_step(u2_ref, u_ref, vt_ref, cnt_ref, w0_ref, r1_ref, e1_ref, acc_scr,
               h_in, h_out, p_in, p_out):
    eb, d = u_ref.shape
    tb = u2_ref.shape[0]
    n_grp = eb // (PEER_IB * P_NKEYS)
    for ib in range(n_grp):
        er = slice(ib * (eb // n_grp), (ib + 1) * (eb // n_grp))
        dr = slice(ib * (d // n_grp), (ib + 1) * (d // n_grp))
        h_out[er, :] = _dot_nt(u_ref[er, :], u2_ref[...])
        acc_scr[dr, :] += _dot(vt_ref[dr, :], p_in[...])
        for tc in range(tb // LANES):
            lanes = slice(tc * LANES, (tc + 1) * LANES)
            for jb in range(P_NKEYS // PEER_JB):
                jrows = slice(jb * PEER_JB, (jb + 1) * PEER_JB)
                gate = [None] * PEER_IB
                for h in range(P_HEADS):
                    r1 = r1_ref[h, jrows, lanes]
                    e1 = e1_ref[h, jrows, lanes]
                    for k in range(PEER_IB):
                        i = ib * PEER_IB + k
                        cnt = _bcast_bf16(cnt_ref[h, i:i + 1, lanes], PEER_JB)
                        w0 = _bcast_bf16(w0_ref[h, i:i + 1, lanes], PEER_JB)
                        term = jnp.where(r1 < cnt, e1, jnp.zeros_like(e1)) * w0
                        gate[k] = term if gate[k] is None else gate[k] + term
                for k in range(PEER_IB):
                    r0 = (ib * PEER_IB + k) * P_NKEYS + jb * PEER_JB
                    hv = h_in[r0:r0 + PEER_JB, lanes]
                    act = hv * (1.0 + lax.erf(hv * (2.0 ** -0.5)))
                    p_out[r0:r0 + PEER_JB, lanes] = act.astype(BF16) * gate[k]


def _peer_kernel(n_e, u2_ref, x1_ref, fg_ref, u_ref, vt_ref, cnt_ref, w0_ref, r1_ref, e1_ref,
                 out_ref, acc_scr, h_scr, p_scr, r1_scr, e1_scr):
    f = pl.program_id(0)

    @pl.when(jnp.maximum(f - 1, 0) % n_e == 0)
    def _():
        r1_scr[...] = r1_ref[...]
        e1_scr[...] = e1_ref[...]

    @pl.when(f == 0)
    def _():
        acc_scr[...] = jnp.zeros(acc_scr.shape, F32)
        h_scr[1] = jnp.zeros(h_scr.shape[1:], F32)
        p_scr[1] = jnp.zeros(p_scr.shape[1:], BF16)

    step = functools.partial(_peer_step, u2_ref, u_ref, vt_ref, cnt_ref, w0_ref, r1_scr, e1_scr,
                             acc_scr)

    @pl.when(f % 2 == 0)
    def _():
        step(h_scr.at[1], h_scr.at[0], p_scr.at[1], p_scr.at[0])

    @pl.when(f % 2 == 1)
    def _():
        step(h_scr.at[0], h_scr.at[1], p_scr.at[0], p_scr.at[1])

    @pl.when((f >= 2) & ((f - 2) % n_e == n_e - 1))
    def _():
        x2 = x1_ref[...] + acc_scr[...].T
        out_ref[...] = (x2 * lax.rsqrt(jnp.mean(x2 * x2, axis=-1, keepdims=True) + EPS)
                        * fg_ref[...])
        acc_scr[...] = jnp.zeros(acc_scr.shape, F32)


def _peer(u2, x1, fg, u_emb, v_t, cnt, w0, r1, e1, tb, eb):
    t, d = u2.shape
    n_e = u_emb.shape[0] // eb
    n_i = eb // P_NKEYS
    n = (t // tb) * n_e
    blk = lambda f, lag: jnp.clip(f - lag, 0, n - 1)
    tok = lambda lag: (lambda f: (blk(f, lag) // n_e, 0))
    gate_blk = lambda f: (0, blk(f, 1) % n_e, blk(f, 1) // n_e)
    key_blk = lambda f: (0, 0, blk(f, 1) // n_e)
    return pl.pallas_call(
        functools.partial(_peer_kernel, n_e),
        out_shape=jax.ShapeDtypeStruct((t, d), F32),
        grid=(n + 2,),
        in_specs=[pl.BlockSpec((tb, d), tok(0)),
                  pl.BlockSpec((tb, d), tok(2)),
                  pl.BlockSpec((1, d), lambda f: (0, 0)),
                  pl.BlockSpec((eb, d), lambda f: (blk(f, 0) % n_e, 0)),
                  pl.BlockSpec((d, eb), lambda f: (0, blk(f, 2) % n_e)),
                  pl.BlockSpec((P_HEADS, n_i, tb), gate_blk),
                  pl.BlockSpec((P_HEADS, n_i, tb), gate_blk),
                  pl.BlockSpec((P_HEADS, P_NKEYS, tb), key_blk),
                  pl.BlockSpec((P_HEADS, P_NKEYS, tb), key_blk)],
        out_specs=pl.BlockSpec((tb, d), tok(2)),
        scratch_shapes=[pltpu.VMEM((d, tb), F32), pltpu.VMEM((2, eb, tb), F32),
                        pltpu.VMEM((2, eb, tb), BF16),
                        pltpu.VMEM((P_HEADS, P_NKEYS, tb), BF16),
                        pltpu.VMEM((P_HEADS, P_NKEYS, tb), BF16)],
        compiler_params=_params("arbitrary"),
        name="peer",
    )(u2, x1, fg, u_emb, v_t, cnt, w0, r1, e1)


def _tiles(t):
    pick = lambda pref: next(c for c in (pref, 512, 256, 128) if c <= pref and t % c == 0)
    return dict(inproj_tm=pick(1024), inproj_tn=1024, merge_tm=pick(512),
                route_tb=pick(256), peer_tb=pick(512), peer_eb=1024)


def kernel(x, norm1_g, w_in, b_in, conv_w, conv_b, m_head_g, lb_table, g_head_g, w_br_m, w_br_g,
           w_out, norm2_g, w_pq, sub_keys, u_emb, v_emb, final_g):
    nb, seq, d = x.shape
    t = nb * seq
    depth = w_in.shape[0]
    assert d == 1024 and seq % M_CHUNK == 0 and t % LANES == 0
    tl = _tiles(t)
    n_gate = 2 * M_HEADS
    g0 = 2 * M_QK + 2 * M_V
    assert (w_in.shape[2] - n_gate) % tl["inproj_tn"] == 0

    lb_all = jnp.cumsum(jax.nn.softmax(lb_table.astype(F32), axis=0), axis=0)
    x2 = x.reshape(t, d)
    row = lambda a: a.reshape(1, -1).astype(F32)

    for l in range(depth):
        w_main = jnp.concatenate([w_in[l][:, :g0], w_in[l][:, g0 + n_gate:]], axis=1).astype(BF16)
        b_main = jnp.concatenate([b_in[l][:g0], b_in[l][g0 + n_gate:]]).reshape(1, -1)
        w_gate = jnp.pad(w_in[l][:, g0:g0 + n_gate], ((0, 0), (0, LANES - n_gate)))
        b_gate = jnp.pad(b_in[l][g0:g0 + n_gate], (0, LANES - n_gate)).reshape(1, -1)

        proj, gates = _inproj(x2, row(norm1_g[l]), w_main, b_main, w_gate, b_gate,
                              tl["inproj_tm"], tl["inproj_tn"])
        y_m = _mlstm(proj, gates, conv_w[l], row(conv_b[l]), row(m_head_g[l]),
                     nb, seq // M_CHUNK)
        y_g = _hgrn(proj, row(lb_all[l]), row(g_head_g[l]), nb, seq // G_CHUNK, g0 // G_W)
        x1, u2 = _merge(x2, y_m, y_g, proj, w_br_m[l].astype(BF16), w_br_g[l].astype(BF16),
                        w_out[l].astype(BF16), row(norm2_g[l]), tl["merge_tm"],
                        (g0 + 4 * G_W) // d)

        keys = sub_keys[l].reshape(2 * P_HEADS, P_NKEYS, -1).astype(BF16)
        cnt, w0, r1, e1 = _route(u2, w_pq[l].T.astype(BF16), keys, tl["route_tb"])
        assert depth == 1
        x2 = _peer(u2, x1, row(final_g), u_emb[l].astype(BF16), v_emb[l].T.astype(BF16),
                   cnt, w0, r1, e1, tl["peer_tb"], tl["peer_eb"])
    return x2.reshape(nb, seq, d)
```

```python
import functools

import jax
import jax.numpy as jnp
from jax import lax
from jax.experimental import pallas as pl
from jax.experimental.pallas import tpu as pltpu

F32 = jnp.float32
BF16 = jnp.bfloat16
HIGHEST = lax.Precision.HIGHEST

EPS = 1e-6
LANES = 128
SUBLANES = 8
VMEM_LIMIT = 48 * 1024 * 1024

M_HEADS, M_QK_DIM, M_V_DIM, M_CHUNK, CONV_W = 4, 256, 512, 128, 4
G_HEADS, G_DIM, G_CHUNK, G_SUB = 8, 128, 64, 16
P_HEADS, P_NKEYS, P_TOPK = 8, 128, 16
M_QK = M_HEADS * M_QK_DIM
M_V = M_HEADS * M_V_DIM
G_W = G_HEADS * G_DIM


def _params(*sem):
    return pltpu.CompilerParams(dimension_semantics=sem, vmem_limit_bytes=VMEM_LIMIT)


def _dot(a, b):
    return jnp.dot(a, b, preferred_element_type=F32)


def _dot_nt(a, b):
    return lax.dot_general(a, b, (((1,), (1,)), ((), ())), preferred_element_type=F32)


def _dot_tn(a, b):
    return lax.dot_general(a, b, (((0,), (0,)), ((), ())), preferred_element_type=F32)


def _sigmoid(x):
    return 1.0 / (1.0 + jnp.exp(-x))


def _tril(n):
    r = lax.broadcasted_iota(jnp.int32, (n, n), 0)
    c = lax.broadcasted_iota(jnp.int32, (n, n), 1)
    return (r >= c)


def _inproj_kernel(x_ref, g_ref, w_ref, b_ref, wg_ref, bg_ref, out_ref, gate_ref, u_scr):
    @pl.when(pl.program_id(1) == 0)
    def _():
        x = x_ref[...]
        u = x * lax.rsqrt(jnp.mean(x * x, axis=-1, keepdims=True) + EPS) * g_ref[...]
        u_scr[...] = u.astype(BF16)
        gate_ref[...] = jnp.dot(u, wg_ref[...], precision=HIGHEST,
                                preferred_element_type=F32) + bg_ref[...]

    out_ref[...] = (_dot(u_scr[...], w_ref[...]) + b_ref[...]).astype(BF16)


def _inproj(x2, g, w, b, wg, bg, tm, tn):
    t, d = x2.shape
    n = w.shape[1]
    return pl.pallas_call(
        _inproj_kernel,
        out_shape=(jax.ShapeDtypeStruct((t, n), BF16), jax.ShapeDtypeStruct((t, LANES), F32)),
        grid=(t // tm, n // tn),
        in_specs=[
            pl.BlockSpec((tm, d), lambda i, j: (i, 0)),
            pl.BlockSpec((1, d), lambda i, j: (0, 0)),
            pl.BlockSpec((d, tn), lambda i, j: (0, j)),
            pl.BlockSpec((1, tn), lambda i, j: (0, j)),
            pl.BlockSpec((d, LANES), lambda i, j: (0, 0)),
            pl.BlockSpec((1, LANES), lambda i, j: (0, 0)),
        ],
        out_specs=(pl.BlockSpec((tm, tn), lambda i, j: (i, j)),
                   pl.BlockSpec((tm, LANES), lambda i, j: (i, 0))),
        scratch_shapes=[pltpu.VMEM((tm, d), BF16)],
        compiler_params=_params("parallel", "arbitrary"),
        name="inproj",
    )(x2, g, w, b, wg, bg)


def _mlstm_kernel(qk_ref, v_ref, o_ref, gt_ref, cw_ref, cb_ref, hg_ref, y_ref,
                  ext_scr, c_scr, m_scr):
    L = M_CHUNK

    @pl.when(pl.program_id(1) == 0)
    def _():
        ext_scr[0:SUBLANES, :] = jnp.zeros((SUBLANES, 2 * M_QK), F32)
        c_scr[...] = jnp.zeros(c_scr.shape, F32)
        m_scr[...] = jnp.zeros(m_scr.shape, F32)

    cur = qk_ref[...].astype(F32)
    ext_scr[SUBLANES:SUBLANES + L, :] = cur
    acc = cb_ref[...] + cw_ref[CONV_W - 1:CONV_W, :] * cur
    for j in range(CONV_W - 1):
        off = SUBLANES - (CONV_W - 1) + j
        acc = acc + cw_ref[j:j + 1, :] * ext_scr[off:off + L, :]
    ext_scr[0:SUBLANES, :] = cur[L - SUBLANES:L, :]
    qk = acc * _sigmoid(acc)

    gt = gt_ref[...]
    lf = jnp.minimum(gt, 0.0) - jnp.log(1.0 + jnp.exp(-jnp.abs(gt)))
    tri = _tril(L)
    bcols = jnp.dot(tri.astype(F32), lf, precision=HIGHEST, preferred_element_type=F32)
    gt_t = gt.T
    b_t = bcols.T
    ones_ext = jnp.ones((L, LANES), BF16)

    for h in range(M_HEADS):
        icol = gt[:, h:h + 1]
        bcol = bcols[:, M_HEADS + h:M_HEADS + h + 1]
        irow = gt_t[h:h + 1, :]
        brow = b_t[M_HEADS + h:M_HEADS + h + 1, :]
        m_prev = m_scr[h, 0:1, 0:1]
        b_last = bcol[L - 1:L, :]

        dmat = jnp.where(tri, bcol - brow + irow, -jnp.inf)
        inter = bcol + m_prev
        m_t = jnp.maximum(inter, jnp.max(dmat, axis=-1, keepdims=True))
        dexp = jnp.exp(dmat - m_t)
        inter_w = jnp.exp(inter - m_t)

        q = (qk[:, h * M_QK_DIM:(h + 1) * M_QK_DIM] * (M_QK_DIM ** -0.5)).astype(BF16)
        kf = qk[:, M_QK + h * M_QK_DIM:M_QK + (h + 1) * M_QK_DIM]
        v_ext = jnp.concatenate([v_ref[:, h * M_V_DIM:(h + 1) * M_V_DIM], ones_ext], axis=1)

        scores = (_dot_nt(q, kf.astype(BF16)) * dexp).astype(BF16)
        c_old = c_scr[h]
        nd = _dot(scores, v_ext) + inter_w * _dot(q, c_old.astype(BF16))
        den = nd[:, M_V_DIM:]
        inv = 1.0 / jnp.maximum(jnp.abs(den), jnp.exp(-m_t))
        hs = [nd[:, j * LANES:(j + 1) * LANES] * inv for j in range(M_V_DIM // LANES)]
        hh = jnp.concatenate(hs, axis=1)

        m_new = m_t[L - 1:L, :]
        w_s = jnp.exp(b_last - bcol + icol - m_new)
        decay = jnp.exp(b_last + m_prev - m_new)
        kw = (kf * w_s).astype(BF16)
        c_scr[h] = decay * c_old + _dot_tn(kw, v_ext)
        m_scr[h] = jnp.broadcast_to(m_new, (SUBLANES, LANES))

        hn = hh * lax.rsqrt(jnp.mean(hh * hh, axis=-1, keepdims=True) + EPS)
        sl = slice(h * M_V_DIM, (h + 1) * M_V_DIM)
        og = _sigmoid(o_ref[:, sl].astype(F32))
        y_ref[:, sl] = (og * hn * hg_ref[:, sl]).astype(BF16)


def _mlstm(proj, gates, conv_w, conv_b, head_g, nb, nc):
    t = proj.shape[0]
    L = M_CHUNK
    row = lambda b, c: b * nc + c
    return pl.pallas_call(
        _mlstm_kernel,
        out_shape=jax.ShapeDtypeStruct((t, M_V), BF16),
        grid=(nb, nc),
        in_specs=[
            pl.BlockSpec((L, 2 * M_QK), lambda b, c: (row(b, c), 0)),
            pl.BlockSpec((L, M_V), lambda b, c: (row(b, c), 1)),
            pl.BlockSpec((L, M_V), lambda b, c: (row(b, c), 2)),
            pl.BlockSpec((L, LANES), lambda b, c: (row(b, c), 0)),
            pl.BlockSpec((CONV_W, 2 * M_QK), lambda b, c: (0, 0)),
            pl.BlockSpec((1, 2 * M_QK), lambda b, c: (0, 0)),
            pl.BlockSpec((1, M_V), lambda b, c: (0, 0)),
        ],
        out_specs=pl.BlockSpec((L, M_V), lambda b, c: (row(b, c), 0)),
        scratch_shapes=[
            pltpu.VMEM((SUBLANES + L, 2 * M_QK), F32),
            pltpu.VMEM((M_HEADS, M_QK_DIM, M_V_DIM + LANES), F32),
            pltpu.VMEM((M_HEADS, SUBLANES, LANES), F32),
        ],
        compiler_params=_params("parallel", "arbitrary"),
        name="mlstm",
    )(proj, proj, proj, gates, conv_w, conv_b, head_g)


def _hgrn_kernel(gf_ref, gi_ref, gq_ref, gg_ref, lb_ref, hg_ref, y_ref, st_scr):
    L, SUB, NSUB = G_CHUNK, G_SUB, G_CHUNK // G_SUB

    @pl.when(pl.program_id(1) == 0)
    def _():
        st_scr[...] = jnp.zeros(st_scr.shape, F32)

    lb = lb_ref[...]
    f = lb + (1.0 - lb) * _sigmoid(gf_ref[...].astype(F32))
    k_all = 1.0 - f
    cum_all = jnp.dot(_tril(L).astype(F32), jnp.log(f), precision=HIGHEST,
                      preferred_element_type=F32)
    gq = gq_ref[...].astype(F32)
    q_all = gq * _sigmoid(gq)

    row = lax.broadcasted_iota(jnp.int32, (L, G_DIM), 0)
    row_sub = lax.broadcasted_iota(jnp.int32, (SUB, G_DIM), 0)
    ones_sq = jnp.ones((G_DIM, G_DIM), BF16)

    for h in range(G_HEADS):
        sl = slice(h * G_DIM, (h + 1) * G_DIM)
        cum, q, k = cum_all[:, sl], q_all[:, sl], k_all[:, sl]
        iv = gi_ref[:, sl]
        ivf = iv.astype(F32)
        ends = [cum[SUB * b + SUB - 1:SUB * b + SUB, :] for b in range(NSUB)]
        last = ends[NSUB - 1]
        end_of_row = jnp.concatenate([jnp.broadcast_to(e, (SUB, G_DIM)) for e in ends], axis=0)

        khat = k * jnp.exp(end_of_row - cum)
        qs, ks = [], []
        for b in range(NSUB - 1):
            lo = SUB * (b + 1)
            qs.append(jnp.where(row >= lo, q * jnp.exp(jnp.minimum(cum - ends[b], 0.0)), 0.0))
            ks.append(jnp.where((row >= lo - SUB) & (row < lo), khat, 0.0))
        att = _dot_nt(jnp.concatenate(qs, axis=1).astype(BF16),
                      jnp.concatenate(ks, axis=1).astype(BF16))
        o = _dot(att.astype(BF16), iv)

        st = st_scr[h]
        o = o + _dot_nt((q * jnp.exp(cum)).astype(BF16), st.astype(BF16))

        o_diag = []
        for a in range(NSUB):
            r0 = SUB * a
            q_a, k_a, c_a, i_a = (z[r0:r0 + SUB, :] for z in (q, k, cum, ivf))
            ms = []
            for s in range(SUB):
                e = jnp.exp(jnp.where(row_sub >= s, c_a - c_a[s:s + 1, :], -jnp.inf))
                ms.append(q_a * (k_a[s:s + 1, :] * e))
            red = _dot(jnp.concatenate(ms, axis=0).astype(BF16), ones_sq)
            acc = red[0:SUB, :] * i_a[0:1, :]
            for s in range(1, SUB):
                acc = acc + red[s * SUB:(s + 1) * SUB, :] * i_a[s:s + 1, :]
            o_diag.append(acc)
        o = o + jnp.concatenate(o_diag, axis=0)

        kd = (k * jnp.exp(last - cum)).astype(BF16)
        st_scr[h] = jnp.exp(last) * st + _dot_tn(iv, kd)

        hn = o * lax.rsqrt(jnp.mean(o * o, axis=-1, keepdims=True) + EPS)
        gg = gg_ref[:, sl].astype(F32)
        y_ref[:, sl] = (hn * hg_ref[:, sl] * (gg * _sigmoid(gg))).astype(BF16)


def _hgrn(proj, lb, head_g, nb, nc, col0):
    t = proj.shape[0]
    L = G_CHUNK
    row = lambda b, c: b * nc + c
    spec = lambda j: pl.BlockSpec((L, G_W), lambda b, c: (row(b, c), col0 + j))
    return pl.pallas_call(
        _hgrn_kernel,
        out_shape=jax.ShapeDtypeStruct((t, G_W), BF16),
        grid=(nb, nc),
        in_specs=[spec(0), spec(1), spec(2), spec(3),
                  pl.BlockSpec((1, G_W), lambda b, c: (0, 0)),
                  pl.BlockSpec((1, G_W), lambda b, c: (0, 0))],
        out_specs=pl.BlockSpec((L, G_W), lambda b, c: (row(b, c), 0)),
        scratch_shapes=[pltpu.VMEM((G_HEADS, G_DIM, G_DIM), F32)],
        compiler_params=_params("parallel", "arbitrary"),
        name="hgrn2",
    )(proj, proj, proj, proj, lb, head_g)


def _merge_kernel(x_ref, ym_ref, yg_ref, am_ref, ag_ref, wm_ref, wg_ref, wo_ref, g2_ref,
                  x1_ref, u2_ref):
    zm = _dot(ym_ref[...], wm_ref[...])
    zg = _dot(yg_ref[...], wg_ref[...])
    z = _sigmoid(am_ref[...].astype(F32)) * zm + _sigmoid(ag_ref[...].astype(F32)) * zg
    x1 = x_ref[...] + _dot(z.astype(BF16), wo_ref[...])
    x1_ref[...] = x1
    u2 = x1 * lax.rsqrt(jnp.mean(x1 * x1, axis=-1, keepdims=True) + EPS) * g2_ref[...]
    u2_ref[...] = u2.astype(BF16)


def _merge(x2, ym, yg, proj, wm, wg, wo, g2, tm, col_am):
    t, d = x2.shape
    rows = lambda w: pl.BlockSpec((tm, w), lambda i: (i, 0))
    full = lambda a: pl.BlockSpec(a.shape, lambda i: (0, 0))
    return pl.pallas_call(
        _merge_kernel,
        out_shape=(jax.ShapeDtypeStruct((t, d), F32), jax.ShapeDtypeStruct((t, d), BF16)),
        grid=(t // tm,),
        in_specs=[rows(d), rows(M_V), rows(G_W),
                  pl.BlockSpec((tm, d), lambda i: (i, col_am)),
                  pl.BlockSpec((tm, d), lambda i: (i, col_am + 1)),
                  full(wm), full(wg), full(wo), full(g2)],
        out_specs=(rows(d), rows(d)),
        compiler_params=_params("parallel"),
        name="merge",
    )(x2, ym, yg, proj, proj, wm, wg, wo, g2)


def _top_rows(vals, k, want_rank=False):
    out = []
    rank = jnp.full(vals.shape, float(k), F32) if want_rank else None
    for r in range(k):
        m = jnp.max(vals, axis=0, keepdims=True)
        out.append(m)
        hit = vals == m
        if want_rank:
            rank = jnp.where(hit, float(r), rank)
        vals = jnp.where(hit, -jnp.inf, vals)
    return out, rank


def _route_kernel(u2_ref, wq_ref, keys_ref, cnt_ref, w0_ref, r1_ref, e1_ref, cand_scr):
    K = P_TOPK
    tb = u2_ref.shape[0]
    q_t = _dot_nt(wq_ref[...], u2_ref[...])
    pairs = [(r, c) for r in range(K) for c in range(K) if (r + 1) * (c + 1) <= K]
    n_pad = cand_scr.shape[0] - len(pairs)
    cand_scr[len(pairs):, :] = jnp.full((n_pad, tb), -jnp.inf, F32)

    for h in range(P_HEADS):
        s = []
        for p in range(2):
            r0 = (2 * h + p) * P_NKEYS
            s.append(_dot(keys_ref[2 * h + p], q_t[r0:r0 + P_NKEYS, :].astype(BF16)))
        a, _ = _top_rows(s[0], K)
        b, rank1 = _top_rows(s[1], K, want_rank=True)
        for n, (r, c) in enumerate(pairs):
            cand_scr[n:n + 1, :] = a[r] + b[c]
        top, _ = _top_rows(cand_scr[...], K)
        z = jnp.ones_like(top[0])
        for r in range(1, K):
            z = z + jnp.exp(top[r] - top[0])
        thr = top[K - 1] - s[0]
        cnt = jnp.zeros_like(thr)
        for c in range(K):
            cnt = cnt + jnp.where(b[c] >= thr, 1.0, 0.0)
        cnt_ref[h] = cnt
        w0_ref[h] = 0.5 * jnp.exp(s[0] - a[0]) / z
        r1_ref[h] = rank1.astype(BF16)
        e1_ref[h] = jnp.exp(s[1] - b[0]).astype(BF16)


def _route(u2, wq_t, keys, tb):
    t, d = u2.shape
    n_pairs = sum(1 for r in range(P_TOPK) for c in range(P_TOPK) if (r + 1) * (c + 1) <= P_TOPK)
    n_cand = -(-n_pairs // SUBLANES) * SUBLANES
    out = lambda dt: jax.ShapeDtypeStruct((P_HEADS, P_NKEYS, t), dt)
    ospec = pl.BlockSpec((P_HEADS, P_NKEYS, tb), lambda i: (0, 0, i))
    return pl.pallas_call(
        _route_kernel,
        out_shape=(out(F32), out(F32), out(BF16), out(BF16)),
        grid=(t // tb,),
        in_specs=[pl.BlockSpec((tb, d), lambda i: (i, 0)),
                  pl.BlockSpec(wq_t.shape, lambda i: (0, 0)),
                  pl.BlockSpec(keys.shape, lambda i: (0, 0, 0))],
        out_specs=(ospec, ospec, ospec, ospec),
        scratch_shapes=[pltpu.VMEM((n_cand, tb), F32)],
        compiler_params=_params("parallel"),
        name="route",
    )(u2, wq_t, keys)


PEER_JB = 64
PEER_IB = 4
BF16_ROWS = 2 * SUBLANES


def _bcast_bf16(row, rows):
    one = jnp.broadcast_to(row, (BF16_ROWS, row.shape[1])).astype(BF16)
    return jnp.concatenate([one] * (rows // BF16_ROWS), axis=0)


def _peer_step(u2_ref, u_ref, vt_ref, cnt_ref, w0_ref, r1_ref, e1_ref, acc_scr,
               h_in, h_out, p_in, p_out):
    eb, d = u_ref.shape
    tb = u2_ref.shape[0]
    n_grp = eb // (PEER_IB * P_NKEYS)
    for ib in range(n_grp):
        er = slice(ib * (eb // n_grp), (ib + 1) * (eb // n_grp))
        dr = slice(ib * (d // n_grp), (ib + 1) * (d // n_grp))
        h_out[er, :] = _dot_nt(u_ref[er, :], u2_ref[...])
        acc_scr[dr, :] += _dot(vt_ref[dr, :], p_in[...])
        for tc in range(tb // LANES):
            lanes = slice(tc * LANES, (tc + 1) * LANES)
            for jb in range(P_NKEYS // PEER_JB):
                jrows = slice(jb * PEER_JB, (jb + 1) * PEER_JB)
                gate = [None] * PEER_IB
                for h in range(P_HEADS):
                    r1 = r1_ref[h, jrows, lanes]
                    e1 = e1_ref[h, jrows, lanes]
                    for k in range(PEER_IB):
                        i = ib * PEER_IB + k
                        cnt = _bcast_bf16(cnt_ref[h, i:i + 1, lanes], PEER_JB)
                        w0 = _bcast_bf16(w0_ref[h, i:i + 1, lanes], PEER_JB)
                        term = jnp.where(r1 < cnt, e1, jnp.zeros_like(e1)) * w0
                        gate[k] = term if gate[k] is None else gate[k] + term
                for k in range(PEER_IB):
                    r0 = (ib * PEER_IB + k) * P_NKEYS + jb * PEER_JB
                    hv = h_in[r0:r0 + PEER_JB, lanes]
                    act = hv * (1.0 + lax.erf(hv * (2.0 ** -0.5)))
                    p_out[r0:r0 + PEER_JB, lanes] = act.astype(BF16) * gate[k]


def _peer_kernel(n_e, u2_ref, x1_ref, fg_ref, u_ref, vt_ref, cnt_ref, w0_ref, r1_ref, e1_ref,
                 out_ref, acc_scr, h_scr, p_scr, r1_scr, e1_scr):
    f = pl.program_id(0)

    @pl.when(jnp.maximum(f - 1, 0) % n_e == 0)
    def _():
        r1_scr[...] = r1_ref[...]
        e1_scr[...] = e1_ref[...]

    @pl.when(f == 0)
    def _():
        acc_scr[...] = jnp.zeros(acc_scr.shape, F32)
        h_scr[1] = jnp.zeros(h_scr.shape[1:], F32)
        p_scr[1] = jnp.zeros(p_scr.shape[1:], BF16)

    step = functools.partial(_peer_step, u2_ref, u_ref, vt_ref, cnt_ref, w0_ref, r1_scr, e1_scr,
                             acc_scr)

    @pl.when(f % 2 == 0)
    def _():
        step(h_scr.at[1], h_scr.at[0], p_scr.at[1], p_scr.at[0])

    @pl.when(f % 2 == 1)
    def _():
        step(h_scr.at[0], h_scr.at[1], p_scr.at[0], p_scr.at[1])

    @pl.when((f >= 2) & ((f - 2) % n_e == n_e - 1))
    def _():
        x2 = x1_ref[...] + acc_scr[...].T
        out_ref[...] = (x2 * lax.rsqrt(jnp.mean(x2 * x2, axis=-1, keepdims=True) + EPS)
                        * fg_ref[...])
        acc_scr[...] = jnp.zeros(acc_scr.shape, F32)


def _peer(u2, x1, fg, u_emb, v_t, cnt, w0, r1, e1, tb, eb):
    t, d = u2.shape
    n_e = u_emb.shape[0] // eb
    n_i = eb // P_NKEYS
    n = (t // tb) * n_e
    blk = lambda f, lag: jnp.clip(f - lag, 0, n - 1)
    tok = lambda lag: (lambda f: (blk(f, lag) // n_e, 0))
    gate_blk = lambda f: (0, blk(f, 1) % n_e, blk(f, 1) // n_e)
    key_blk = lambda f: (0, 0, blk(f, 1) // n_e)
    return pl.pallas_call(
        functools.partial(_peer_kernel, n_e),
        out_shape=jax.ShapeDtypeStruct((t, d), F32),
        grid=(n + 2,),
        in_specs=[pl.BlockSpec((tb, d), tok(0)),
                  pl.BlockSpec((tb, d), tok(2)),
                  pl.BlockSpec((1, d), lambda f: (0, 0)),
                  pl.BlockSpec((eb, d), lambda f: (blk(f, 0) % n_e, 0)),
                  pl.BlockSpec((None, d, eb), lambda f: (blk(f, 2) % n_e, 0, 0)),
                  pl.BlockSpec((P_HEADS, n_i, tb), gate_blk),
                  pl.BlockSpec((P_HEADS, n_i, tb), gate_blk),
                  pl.BlockSpec((P_HEADS, P_NKEYS, tb), key_blk),
                  pl.BlockSpec((P_HEADS, P_NKEYS, tb), key_blk)],
        out_specs=pl.BlockSpec((tb, d), tok(2)),
        scratch_shapes=[pltpu.VMEM((d, tb), F32), pltpu.VMEM((2, eb, tb), F32),
                        pltpu.VMEM((2, eb, tb), BF16),
                        pltpu.VMEM((P_HEADS, P_NKEYS, tb), BF16),
                        pltpu.VMEM((P_HEADS, P_NKEYS, tb), BF16)],
        compiler_params=_params("arbitrary"),
        name="peer",
    )(u2, x1, fg, u_emb, v_t, cnt, w0, r1, e1)


def _tiles(t):
    pick = lambda pref: next(c for c in (pref, 512, 256, 128) if c <= pref and t % c == 0)
    return dict(inproj_tm=pick(1024), inproj_tn=1024, merge_tm=pick(512),
                route_tb=pick(256), peer_tb=pick(512), peer_eb=1024)


def kernel(x, norm1_g, w_in, b_in, conv_w, conv_b, m_head_g, lb_table, g_head_g, w_br_m, w_br_g,
           w_out, norm2_g, w_pq, sub_keys, u_emb, v_emb, final_g):
    nb, seq, d = x.shape
    t = nb * seq
    depth = w_in.shape[0]
    assert d == 1024 and seq % M_CHUNK == 0 and t % LANES == 0
    tl = _tiles(t)
    n_gate = 2 * M_HEADS
    g0 = 2 * M_QK + 2 * M_V
    assert (w_in.shape[2] - n_gate) % tl["inproj_tn"] == 0

    lb_all = jnp.cumsum(jax.nn.softmax(lb_table.astype(F32), axis=0), axis=0)
    x2 = x.reshape(t, d)
    row = lambda a: a.reshape(1, -1).astype(F32)

    for l in range(depth):
        w_main = jnp.concatenate([w_in[l][:, :g0], w_in[l][:, g0 + n_gate:]], axis=1).astype(BF16)
        b_main = jnp.concatenate([b_in[l][:g0], b_in[l][g0 + n_gate:]]).reshape(1, -1)
        w_gate = jnp.pad(w_in[l][:, g0:g0 + n_gate], ((0, 0), (0, LANES - n_gate)))
        b_gate = jnp.pad(b_in[l][g0:g0 + n_gate], (0, LANES - n_gate)).reshape(1, -1)

        proj, gates = _inproj(x2, row(norm1_g[l]), w_main, b_main, w_gate, b_gate,
                              tl["inproj_tm"], tl["inproj_tn"])
        y_m = _mlstm(proj, gates, conv_w[l], row(conv_b[l]), row(m_head_g[l]),
                     nb, seq // M_CHUNK)
        y_g = _hgrn(proj, row(lb_all[l]), row(g_head_g[l]), nb, seq // G_CHUNK, g0 // G_W)
        x1, u2 = _merge(x2, y_m, y_g, proj, w_br_m[l].astype(BF16), w_br_g[l].astype(BF16),
                        w_out[l].astype(BF16), row(norm2_g[l]), tl["merge_tm"],
                        (g0 + 4 * G_W) // d)

        keys = sub_keys[l].reshape(2 * P_HEADS, P_NKEYS, -1).astype(BF16)
        cnt, w0, r1, e1 = _route(u2, w_pq[l].T.astype(BF16), keys, tl["route_tb"])
        assert depth == 1
        v_blocks = v_emb[l].astype(BF16).reshape(-1, tl["peer_eb"], d).transpose(0, 2, 1)
        x2 = _peer(u2, x1, row(final_g), u_emb[l].astype(BF16), v_blocks,
                   cnt, w0, r1, e1, tl["peer_tb"], tl["peer_eb"])
    return x2.reshape(nb, seq, d)
```

```python
import functools

import jax
import jax.numpy as jnp
from jax import lax
from jax.experimental import pallas as pl
from jax.experimental.pallas import tpu as pltpu

F32 = jnp.float32
BF16 = jnp.bfloat16
HIGHEST = lax.Precision.HIGHEST

EPS = 1e-6
LANES = 128
SUBLANES = 8
VMEM_LIMIT = 48 * 1024 * 1024

M_HEADS, M_QK_DIM, M_V_DIM, M_CHUNK, CONV_W = 4, 256, 512, 128, 4
G_HEADS, G_DIM, G_CHUNK, G_SUB = 8, 128, 64, 16
P_HEADS, P_NKEYS, P_TOPK = 8, 128, 16
M_QK = M_HEADS * M_QK_DIM
M_V = M_HEADS * M_V_DIM
G_W = G_HEADS * G_DIM


def _params(*sem):
    return pltpu.CompilerParams(dimension_semantics=sem, vmem_limit_bytes=VMEM_LIMIT)


def _dot(a, b):
    return jnp.dot(a, b, preferred_element_type=F32)


def _dot_nt(a, b):
    return lax.dot_general(a, b, (((1,), (1,)), ((), ())), preferred_element_type=F32)


def _dot_tn(a, b):
    return lax.dot_general(a, b, (((0,), (0,)), ((), ())), preferred_element_type=F32)


def _sigmoid(x):
    return 1.0 / (1.0 + jnp.exp(-x))


def _tril(n):
    r = lax.broadcasted_iota(jnp.int32, (n, n), 0)
    c = lax.broadcasted_iota(jnp.int32, (n, n), 1)
    return (r >= c)


def _inproj_kernel(x_ref, g_ref, w_ref, b_ref, wg_ref, bg_ref, out_ref, gate_ref, u_scr):
    @pl.when(pl.program_id(1) == 0)
    def _():
        x = x_ref[...]
        u = x * lax.rsqrt(jnp.mean(x * x, axis=-1, keepdims=True) + EPS) * g_ref[...]
        u_scr[...] = u.astype(BF16)
        gate_ref[...] = jnp.dot(u, wg_ref[...], precision=HIGHEST,
                                preferred_element_type=F32) + bg_ref[...]

    out_ref[...] = (_dot(u_scr[...], w_ref[...]) + b_ref[...]).astype(BF16)


def _inproj(x2, g, w, b, wg, bg, tm, tn):
    t, d = x2.shape
    n = w.shape[1]
    return pl.pallas_call(
        _inproj_kernel,
        out_shape=(jax.ShapeDtypeStruct((t, n), BF16), jax.ShapeDtypeStruct((t, LANES), F32)),
        grid=(t // tm, n // tn),
        in_specs=[
            pl.BlockSpec((tm, d), lambda i, j: (i, 0)),
            pl.BlockSpec((1, d), lambda i, j: (0, 0)),
            pl.BlockSpec((d, tn), lambda i, j: (0, j)),
            pl.BlockSpec((1, tn), lambda i, j: (0, j)),
            pl.BlockSpec((d, LANES), lambda i, j: (0, 0)),
            pl.BlockSpec((1, LANES), lambda i, j: (0, 0)),
        ],
        out_specs=(pl.BlockSpec((tm, tn), lambda i, j: (i, j)),
                   pl.BlockSpec((tm, LANES), lambda i, j: (i, 0))),
        scratch_shapes=[pltpu.VMEM((tm, d), BF16)],
        compiler_params=_params("parallel", "arbitrary"),
        name="inproj",
    )(x2, g, w, b, wg, bg)


def _mlstm_kernel(qk_ref, v_ref, o_ref, gt_ref, cw_ref, cb_ref, hg_ref, y_ref,
                  ext_scr, c_scr, m_scr):
    L = M_CHUNK

    @pl.when(pl.program_id(1) == 0)
    def _():
        ext_scr[0:SUBLANES, :] = jnp.zeros((SUBLANES, 2 * M_QK), F32)
        c_scr[...] = jnp.zeros(c_scr.shape, F32)
        m_scr[...] = jnp.zeros(m_scr.shape, F32)

    cur = qk_ref[...].astype(F32)
    ext_scr[SUBLANES:SUBLANES + L, :] = cur
    acc = cb_ref[...] + cw_ref[CONV_W - 1:CONV_W, :] * cur
    for j in range(CONV_W - 1):
        off = SUBLANES - (CONV_W - 1) + j
        acc = acc + cw_ref[j:j + 1, :] * ext_scr[off:off + L, :]
    ext_scr[0:SUBLANES, :] = cur[L - SUBLANES:L, :]
    qk = acc * _sigmoid(acc)

    gt = gt_ref[...]
    lf = jnp.minimum(gt, 0.0) - jnp.log(1.0 + jnp.exp(-jnp.abs(gt)))
    tri = _tril(L)
    bcols = jnp.dot(tri.astype(F32), lf, precision=HIGHEST, preferred_element_type=F32)
    gt_t = gt.T
    b_t = bcols.T
    ones_ext = jnp.ones((L, LANES), BF16)

    for h in range(M_HEADS):
        icol = gt[:, h:h + 1]
        bcol = bcols[:, M_HEADS + h:M_HEADS + h + 1]
        irow = gt_t[h:h + 1, :]
        brow = b_t[M_HEADS + h:M_HEADS + h + 1, :]
        m_prev = m_scr[h, 0:1, 0:1]
        b_last = bcol[L - 1:L, :]

        dmat = jnp.where(tri, bcol - brow + irow, -jnp.inf)
        inter = bcol + m_prev
        m_t = jnp.maximum(inter, jnp.max(dmat, axis=-1, keepdims=True))
        dexp = jnp.exp(dmat - m_t)
        inter_w = jnp.exp(inter - m_t)

        q = (qk[:, h * M_QK_DIM:(h + 1) * M_QK_DIM] * (M_QK_DIM ** -0.5)).astype(BF16)
        kf = qk[:, M_QK + h * M_QK_DIM:M_QK + (h + 1) * M_QK_DIM]
        v_ext = jnp.concatenate([v_ref[:, h * M_V_DIM:(h + 1) * M_V_DIM], ones_ext], axis=1)

        scores = (_dot_nt(q, kf.astype(BF16)) * dexp).astype(BF16)
        c_old = c_scr[h]
        nd = _dot(scores, v_ext) + inter_w * _dot(q, c_old.astype(BF16))
        den = nd[:, M_V_DIM:]
        inv = 1.0 / jnp.maximum(jnp.abs(den), jnp.exp(-m_t))
        hs = [nd[:, j * LANES:(j + 1) * LANES] * inv for j in range(M_V_DIM // LANES)]
        hh = jnp.concatenate(hs, axis=1)

        m_new = m_t[L - 1:L, :]
        w_s = jnp.exp(b_last - bcol + icol - m_new)
        decay = jnp.exp(b_last + m_prev - m_new)
        kw = (kf * w_s).astype(BF16)
        c_scr[h] = decay * c_old + _dot_tn(kw, v_ext)
        m_scr[h] = jnp.broadcast_to(m_new, (SUBLANES, LANES))

        hn = hh * lax.rsqrt(jnp.mean(hh * hh, axis=-1, keepdims=True) + EPS)
        sl = slice(h * M_V_DIM, (h + 1) * M_V_DIM)
        og = _sigmoid(o_ref[:, sl].astype(F32))
        y_ref[:, sl] = (og * hn * hg_ref[:, sl]).astype(BF16)


def _mlstm(proj, gates, conv_w, conv_b, head_g, nb, nc):
    t = proj.shape[0]
    L = M_CHUNK
    row = lambda b, c: b * nc + c
    return pl.pallas_call(
        _mlstm_kernel,
        out_shape=jax.ShapeDtypeStruct((t, M_V), BF16),
        grid=(nb, nc),
        in_specs=[
            pl.BlockSpec((L, 2 * M_QK), lambda b, c: (row(b, c), 0)),
            pl.BlockSpec((L, M_V), lambda b, c: (row(b, c), 1)),
            pl.BlockSpec((L, M_V), lambda b, c: (row(b, c), 2)),
            pl.BlockSpec((L, LANES), lambda b, c: (row(b, c), 0)),
            pl.BlockSpec((CONV_W, 2 * M_QK), lambda b, c: (0, 0)),
            pl.BlockSpec((1, 2 * M_QK), lambda b, c: (0, 0)),
            pl.BlockSpec((1, M_V), lambda b, c: (0, 0)),
        ],
        out_specs=pl.BlockSpec((L, M_V), lambda b, c: (row(b, c), 0)),
        scratch_shapes=[
            pltpu.VMEM((SUBLANES + L, 2 * M_QK), F32),
            pltpu.VMEM((M_HEADS, M_QK_DIM, M_V_DIM + LANES), F32),
            pltpu.VMEM((M_HEADS, SUBLANES, LANES), F32),
        ],
        compiler_params=_params("parallel", "arbitrary"),
        name="mlstm",
    )(proj, proj, proj, gates, conv_w, conv_b, head_g)


def _hgrn_kernel(gf_ref, gi_ref, gq_ref, gg_ref, lb_ref, hg_ref, y_ref, st_scr):
    L, SUB, NSUB = G_CHUNK, G_SUB, G_CHUNK // G_SUB

    @pl.when(pl.program_id(1) == 0)
    def _():
        st_scr[...] = jnp.zeros(st_scr.shape, F32)

    lb = lb_ref[...]
    f = lb + (1.0 - lb) * _sigmoid(gf_ref[...].astype(F32))
    k_all = 1.0 - f
    cum_all = jnp.dot(_tril(L).astype(F32), jnp.log(f), precision=HIGHEST,
                      preferred_element_type=F32)
    gq = gq_ref[...].astype(F32)
    q_all = gq * _sigmoid(gq)

    row = lax.broadcasted_iota(jnp.int32, (L, G_DIM), 0)
    row_sub = lax.broadcasted_iota(jnp.int32, (SUB, G_DIM), 0)
    ones_sq = jnp.ones((G_DIM, G_DIM), BF16)

    for h in range(G_HEADS):
        sl = slice(h * G_DIM, (h + 1) * G_DIM)
        cum, q, k = cum_all[:, sl], q_all[:, sl], k_all[:, sl]
        iv = gi_ref[:, sl]
        ivf = iv.astype(F32)
        ends = [cum[SUB * b + SUB - 1:SUB * b + SUB, :] for b in range(NSUB)]
        last = ends[NSUB - 1]
        end_of_row = jnp.concatenate([jnp.broadcast_to(e, (SUB, G_DIM)) for e in ends], axis=0)

        khat = k * jnp.exp(end_of_row - cum)
        qs, ks = [], []
        for b in range(NSUB - 1):
            lo = SUB * (b + 1)
            qs.append(jnp.where(row >= lo, q * jnp.exp(jnp.minimum(cum - ends[b], 0.0)), 0.0))
            ks.append(jnp.where((row >= lo - SUB) & (row < lo), khat, 0.0))
        att = _dot_nt(jnp.concatenate(qs, axis=1).astype(BF16),
                      jnp.concatenate(ks, axis=1).astype(BF16))
        o = _dot(att.astype(BF16), iv)

        st = st_scr[h]
        o = o + _dot_nt((q * jnp.exp(cum)).astype(BF16), st.astype(BF16))

        o_diag = []
        for a in range(NSUB):
            r0 = SUB * a
            q_a, k_a, c_a, i_a = (z[r0:r0 + SUB, :] for z in (q, k, cum, ivf))
            ms = []
            for s in range(SUB):
                e = jnp.exp(jnp.where(row_sub >= s, c_a - c_a[s:s + 1, :], -jnp.inf))
                ms.append(q_a * (k_a[s:s + 1, :] * e))
            red = _dot(jnp.concatenate(ms, axis=0).astype(BF16), ones_sq)
            acc = red[0:SUB, :] * i_a[0:1, :]
            for s in range(1, SUB):
                acc = acc + red[s * SUB:(s + 1) * SUB, :] * i_a[s:s + 1, :]
            o_diag.append(acc)
        o = o + jnp.concatenate(o_diag, axis=0)

        kd = (k * jnp.exp(last - cum)).astype(BF16)
        st_scr[h] = jnp.exp(last) * st + _dot_tn(iv, kd)

        hn = o * lax.rsqrt(jnp.mean(o * o, axis=-1, keepdims=True) + EPS)
        gg = gg_ref[:, sl].astype(F32)
        y_ref[:, sl] = (hn * hg_ref[:, sl] * (gg * _sigmoid(gg))).astype(BF16)


def _hgrn(proj, lb, head_g, nb, nc, col0):
    t = proj.shape[0]
    L = G_CHUNK
    row = lambda b, c: b * nc + c
    spec = lambda j: pl.BlockSpec((L, G_W), lambda b, c: (row(b, c), col0 + j))
    return pl.pallas_call(
        _hgrn_kernel,
        out_shape=jax.ShapeDtypeStruct((t, G_W), BF16),
        grid=(nb, nc),
        in_specs=[spec(0), spec(1), spec(2), spec(3),
                  pl.BlockSpec((1, G_W), lambda b, c: (0, 0)),
                  pl.BlockSpec((1, G_W), lambda b, c: (0, 0))],
        out_specs=pl.BlockSpec((L, G_W), lambda b, c: (row(b, c), 0)),
        scratch_shapes=[pltpu.VMEM((G_HEADS, G_DIM, G_DIM), F32)],
        compiler_params=_params("parallel", "arbitrary"),
        name="hgrn2",
    )(proj, proj, proj, proj, lb, head_g)


def _merge_kernel(x_ref, ym_ref, yg_ref, am_ref, ag_ref, wm_ref, wg_ref, wo_ref, g2_ref,
                  x1_ref, u2_ref):
    zm = _dot(ym_ref[...], wm_ref[...])
    zg = _dot(yg_ref[...], wg_ref[...])
    z = _sigmoid(am_ref[...].astype(F32)) * zm + _sigmoid(ag_ref[...].astype(F32)) * zg
    x1 = x_ref[...] + _dot(z.astype(BF16), wo_ref[...])
    x1_ref[...] = x1
    u2 = x1 * lax.rsqrt(jnp.mean(x1 * x1, axis=-1, keepdims=True) + EPS) * g2_ref[...]
    u2_ref[...] = u2.astype(BF16)


def _merge(x2, ym, yg, proj, wm, wg, wo, g2, tm, col_am):
    t, d = x2.shape
    rows = lambda w: pl.BlockSpec((tm, w), lambda i: (i, 0))
    full = lambda a: pl.BlockSpec(a.shape, lambda i: (0, 0))
    return pl.pallas_call(
        _merge_kernel,
        out_shape=(jax.ShapeDtypeStruct((t, d), F32), jax.ShapeDtypeStruct((t, d), BF16)),
        grid=(t // tm,),
        in_specs=[rows(d), rows(M_V), rows(G_W),
                  pl.BlockSpec((tm, d), lambda i: (i, col_am)),
                  pl.BlockSpec((tm, d), lambda i: (i, col_am + 1)),
                  full(wm), full(wg), full(wo), full(g2)],
        out_specs=(rows(d), rows(d)),
        compiler_params=_params("parallel"),
        name="merge",
    )(x2, ym, yg, proj, proj, wm, wg, wo, g2)


def _sort16_pairs():
    def merge(lo, hi, r):
        step = r * 2
        if step < hi - lo:
            yield from merge(lo, hi, step)
            yield from merge(lo + r, hi, step)
            yield from [(i, i + r) for i in range(lo + r, hi - r, step)]
        else:
            yield (lo, lo + r)

    def sort(lo, hi):
        if hi - lo >= 1:
            mid = lo + (hi - lo) // 2
            yield from sort(lo, mid)
            yield from sort(mid + 1, hi)
            yield from merge(lo, hi, 1)

    return list(sort(0, P_TOPK - 1))


def _top_rows(vals, k):
    out = []
    for _ in range(k):
        m = jnp.max(vals, axis=0, keepdims=True)
        out.append(m)
        vals = jnp.where(vals == m, -jnp.inf, vals)
    return out


def _top16_of_128(vals):
    K = P_TOPK
    v = [vals[SUBLANES * g:SUBLANES * (g + 1), :] for g in range(K)]
    for a, b in _sort16_pairs():
        v[a], v[b] = jnp.maximum(v[a], v[b]), jnp.minimum(v[a], v[b])
    out = []
    for r in range(K):
        m = jnp.max(v[0], axis=0, keepdims=True)
        out.append(m)
        if r == K - 1:
            break
        hit = v[0] == m
        depth = K - 1 - r
        for i in range(depth):
            v[i] = jnp.where(hit, v[i + 1], v[i])
    return out


def _route_kernel(u2_ref, wq_ref, keys_ref, thr_ref, w0_ref, s1_ref, e1_ref, cand_scr):
    K = P_TOPK
    tb = u2_ref.shape[0]
    q_t = _dot_nt(wq_ref[...], u2_ref[...])
    pairs = [(r, c) for r in range(K) for c in range(K) if (r + 1) * (c + 1) <= K]
    n_pad = cand_scr.shape[0] - len(pairs)
    cand_scr[len(pairs):, :] = jnp.full((n_pad, tb), -jnp.inf, F32)

    for h in range(P_HEADS):
        s = []
        for p in range(2):
            r0 = (2 * h + p) * P_NKEYS
            s.append(_dot(keys_ref[2 * h + p], q_t[r0:r0 + P_NKEYS, :].astype(BF16)))
        a = _top16_of_128(s[0])
        b = _top16_of_128(s[1])
        for n, (r, c) in enumerate(pairs):
            cand_scr[n:n + 1, :] = a[r] + b[c]
        top = _top_rows(cand_scr[...], K)
        z = jnp.ones_like(top[0])
        for r in range(1, K):
            z = z + jnp.exp(top[r] - top[0])
        thr_ref[h] = top[K - 1] - s[0]
        w0_ref[h] = 0.5 * jnp.exp(s[0] - a[0]) / z
        s1_ref[h] = s[1]
        e1_ref[h] = jnp.exp(s[1] - b[0])


def _route(u2, wq_t, keys, tb):
    t, d = u2.shape
    n_pairs = sum(1 for r in range(P_TOPK) for c in range(P_TOPK) if (r + 1) * (c + 1) <= P_TOPK)
    n_cand = -(-n_pairs // SUBLANES) * SUBLANES
    out = jax.ShapeDtypeStruct((P_HEADS, P_NKEYS, t), F32)
    ospec = pl.BlockSpec((P_HEADS, P_NKEYS, tb), lambda i: (0, 0, i))
    return pl.pallas_call(
        _route_kernel,
        out_shape=(out, out, out, out),
        grid=(t // tb,),
        in_specs=[pl.BlockSpec((tb, d), lambda i: (i, 0)),
                  pl.BlockSpec(wq_t.shape, lambda i: (0, 0)),
                  pl.BlockSpec(keys.shape, lambda i: (0, 0, 0))],
        out_specs=(ospec, ospec, ospec, ospec),
        scratch_shapes=[pltpu.VMEM((n_cand, tb), F32)],
        compiler_params=_params("parallel"),
        name="route",
    )(u2, wq_t, keys)


PEER_JB = 32
PEER_IB = 4


def _peer_step(u2_ref, u_ref, vt_ref, thr_ref, w0_ref, s1_ref, e1_ref, acc_scr,
               h_in, h_out, p_in, p_out):
    eb, d = u_ref.shape
    tb = u2_ref.shape[0]
    n_grp = eb // (PEER_IB * P_NKEYS)
    for ib in range(n_grp):
        er = slice(ib * (eb // n_grp), (ib + 1) * (eb // n_grp))
        dr = slice(ib * (d // n_grp), (ib + 1) * (d // n_grp))
        h_out[er, :] = _dot_nt(u_ref[er, :], u2_ref[...])
        acc_scr[dr, :] += _dot(vt_ref[dr, :], p_in[...])
        for tc in range(tb // LANES):
            lanes = slice(tc * LANES, (tc + 1) * LANES)
            for jb in range(P_NKEYS // PEER_JB):
                jrows = slice(jb * PEER_JB, (jb + 1) * PEER_JB)
                gate = [None] * PEER_IB
                for h in range(P_HEADS):
                    s1 = s1_ref[h, jrows, lanes]
                    e1 = e1_ref[h, jrows, lanes]
                    for k in range(PEER_IB):
                        i = ib * PEER_IB + k
                        term = (jnp.where(s1 >= thr_ref[h, i:i + 1, lanes], e1, 0.0)
                                * w0_ref[h, i:i + 1, lanes])
                        gate[k] = term if gate[k] is None else gate[k] + term
                for k in range(PEER_IB):
                    r0 = (ib * PEER_IB + k) * P_NKEYS + jb * PEER_JB
                    hv = h_in[r0:r0 + PEER_JB, lanes]
                    act = hv * (1.0 + lax.erf(hv * (2.0 ** -0.5)))
                    p_out[r0:r0 + PEER_JB, lanes] = (act * gate[k]).astype(BF16)


def _peer_kernel(n_e, u2_ref, x1_ref, fg_ref, u_ref, vt_ref, thr_ref, w0_ref, s1_ref, e1_ref,
                 out_ref, acc_scr, h_scr, p_scr):
    f = pl.program_id(0)

    @pl.when(f == 0)
    def _():
        acc_scr[...] = jnp.zeros(acc_scr.shape, F32)
        h_scr[1] = jnp.zeros(h_scr.shape[1:], F32)
        p_scr[1] = jnp.zeros(p_scr.shape[1:], BF16)

    step = functools.partial(_peer_step, u2_ref, u_ref, vt_ref, thr_ref, w0_ref, s1_ref, e1_ref,
                             acc_scr)

    @pl.when(f % 2 == 0)
    def _():
        step(h_scr.at[1], h_scr.at[0], p_scr.at[1], p_scr.at[0])

    @pl.when(f % 2 == 1)
    def _():
        step(h_scr.at[0], h_scr.at[1], p_scr.at[0], p_scr.at[1])

    @pl.when((f >= 2) & ((f - 2) % n_e == n_e - 1))
    def _():
        x2 = x1_ref[...] + acc_scr[...].T
        out_ref[...] = (x2 * lax.rsqrt(jnp.mean(x2 * x2, axis=-1, keepdims=True) + EPS)
                        * fg_ref[...])
        acc_scr[...] = jnp.zeros(acc_scr.shape, F32)


def _peer(u2, x1, fg, u_emb, v_t, thr, w0, s1, e1, tb, eb):
    t, d = u2.shape
    n_e = u_emb.shape[0] // eb
    n_i = eb // P_NKEYS
    n = (t // tb) * n_e
    blk = lambda f, lag: jnp.clip(f - lag, 0, n - 1)
    tok = lambda lag: (lambda f: (blk(f, lag) // n_e, 0))
    gate_blk = lambda f: (0, blk(f, 1) % n_e, blk(f, 1) // n_e)
    key_blk = lambda f: (0, 0, blk(f, 1) // n_e)
    return pl.pallas_call(
        functools.partial(_peer_kernel, n_e),
        out_shape=jax.ShapeDtypeStruct((t, d), F32),
        grid=(n + 2,),
        in_specs=[pl.BlockSpec((tb, d), tok(0)),
                  pl.BlockSpec((tb, d), tok(2)),
                  pl.BlockSpec((1, d), lambda f: (0, 0)),
                  pl.BlockSpec((eb, d), lambda f: (blk(f, 0) % n_e, 0)),
                  pl.BlockSpec((d, eb), lambda f: (0, blk(f, 2) % n_e)),
                  pl.BlockSpec((P_HEADS, n_i, tb), gate_blk),
                  pl.BlockSpec((P_HEADS, n_i, tb), gate_blk),
                  pl.BlockSpec((P_HEADS, P_NKEYS, tb), key_blk),
                  pl.BlockSpec((P_HEADS, P_NKEYS, tb), key_blk)],
        out_specs=pl.BlockSpec((tb, d), tok(2)),
        scratch_shapes=[pltpu.VMEM((d, tb), F32), pltpu.VMEM((2, eb, tb), F32),
                        pltpu.VMEM((2, eb, tb), BF16)],
        compiler_params=_params("arbitrary"),
        name="peer",
    )(u2, x1, fg, u_emb, v_t, thr, w0, s1, e1)


def _tiles(t):
    pick = lambda pref: next(c for c in (pref, 512, 256, 128) if c <= pref and t % c == 0)
    return dict(inproj_tm=pick(1024), inproj_tn=1024, merge_tm=pick(512),
                route_tb=pick(256), peer_tb=pick(512), peer_eb=1024)


def kernel(x, norm1_g, w_in, b_in, conv_w, conv_b, m_head_g, lb_table, g_head_g, w_br_m, w_br_g,
           w_out, norm2_g, w_pq, sub_keys, u_emb, v_emb, final_g):
    nb, seq, d = x.shape
    t = nb * seq
    depth = w_in.shape[0]
    assert d == 1024 and seq % M_CHUNK == 0 and t % LANES == 0
    tl = _tiles(t)
    n_gate = 2 * M_HEADS
    g0 = 2 * M_QK + 2 * M_V
    assert (w_in.shape[2] - n_gate) % tl["inproj_tn"] == 0

    lb_all = jnp.cumsum(jax.nn.softmax(lb_table.astype(F32), axis=0), axis=0)
    x2 = x.reshape(t, d)
    row = lambda a: a.reshape(1, -1).astype(F32)

    for l in range(depth):
        w_main = jnp.concatenate([w_in[l][:, :g0], w_in[l][:, g0 + n_gate:]], axis=1).astype(BF16)
        b_main = jnp.concatenate([b_in[l][:g0], b_in[l][g0 + n_gate:]]).reshape(1, -1)
        w_gate = jnp.pad(w_in[l][:, g0:g0 + n_gate], ((0, 0), (0, LANES - n_gate)))
        b_gate = jnp.pad(b_in[l][g0:g0 + n_gate], (0, LANES - n_gate)).reshape(1, -1)

        proj, gates = _inproj(x2, row(norm1_g[l]), w_main, b_main, w_gate, b_gate,
                              tl["inproj_tm"], tl["inproj_tn"])
        y_m = _mlstm(proj, gates, conv_w[l], row(conv_b[l]), row(m_head_g[l]),
                     nb, seq // M_CHUNK)
        y_g = _hgrn(proj, row(lb_all[l]), row(g_head_g[l]), nb, seq // G_CHUNK, g0 // G_W)
        x1, u2 = _merge(x2, y_m, y_g, proj, w_br_m[l].astype(BF16), w_br_g[l].astype(BF16),
                        w_out[l].astype(BF16), row(norm2_g[l]), tl["merge_tm"],
                        (g0 + 4 * G_W) // d)

        keys = sub_keys[l].reshape(2 * P_HEADS, P_NKEYS, -1).astype(BF16)
        thr, w0, s1, e1 = _route(u2, w_pq[l].T.astype(BF16), keys, tl["route_tb"])
        assert depth == 1
        x2 = _peer(u2, x1, row(final_g), u_emb[l].astype(BF16), v_emb[l].T.astype(BF16),
                   thr, w0, s1, e1, tl["peer_tb"], tl["peer_eb"])
    return x2.reshape(nb, seq, d)
```

```python
import functools

import jax
import jax.numpy as jnp
from jax import lax
from jax.experimental import pallas as pl
from jax.experimental.pallas import tpu as pltpu

F32 = jnp.float32
BF16 = jnp.bfloat16
HIGHEST = lax.Precision.HIGHEST

EPS = 1e-6
LOG2E = 1.4426950408889634
LANES = 128
SUBLANES = 8
VMEM_LIMIT = 48 * 1024 * 1024

M_HEADS, M_QK_DIM, M_V_DIM, M_CHUNK, CONV_W = 4, 256, 512, 128, 4
G_HEADS, G_DIM, G_CHUNK, G_SUB = 8, 128, 64, 16
P_HEADS, P_NKEYS, P_TOPK = 8, 128, 16
M_QK = M_HEADS * M_QK_DIM
M_V = M_HEADS * M_V_DIM
G_W = G_HEADS * G_DIM


def _params(*sem):
    return pltpu.CompilerParams(dimension_semantics=sem, vmem_limit_bytes=VMEM_LIMIT)


def _dot(a, b):
    return jnp.dot(a, b, preferred_element_type=F32)


def _dot_nt(a, b):
    return lax.dot_general(a, b, (((1,), (1,)), ((), ())), preferred_element_type=F32)


def _dot_tn(a, b):
    return lax.dot_general(a, b, (((0,), (0,)), ((), ())), preferred_element_type=F32)


def _sigmoid(x):
    return 1.0 / (1.0 + jnp.exp2(x * (-LOG2E)))


def _tril(n):
    r = lax.broadcasted_iota(jnp.int32, (n, n), 0)
    c = lax.broadcasted_iota(jnp.int32, (n, n), 1)
    return (r >= c)


def _inproj_kernel(x_ref, g_ref, w_ref, b_ref, wg_ref, bg_ref, out_ref, gate_ref, u_scr):
    @pl.when(pl.program_id(1) == 0)
    def _():
        x = x_ref[...]
        u = x * lax.rsqrt(jnp.mean(x * x, axis=-1, keepdims=True) + EPS) * g_ref[...]
        u_scr[...] = u.astype(BF16)
        gate_ref[...] = _dot(u.astype(BF16), wg_ref[...]) + bg_ref[...]

    out_ref[...] = (_dot(u_scr[...], w_ref[...]) + b_ref[...]).astype(BF16)


def _inproj(x2, g, w, b, wg, bg, tm, tn):
    t, d = x2.shape
    n = w.shape[1]
    return pl.pallas_call(
        _inproj_kernel,
        out_shape=(jax.ShapeDtypeStruct((t, n), BF16), jax.ShapeDtypeStruct((t, LANES), F32)),
        grid=(t // tm, n // tn),
        in_specs=[
            pl.BlockSpec((tm, d), lambda i, j: (i, 0)),
            pl.BlockSpec((1, d), lambda i, j: (0, 0)),
            pl.BlockSpec((d, tn), lambda i, j: (0, j)),
            pl.BlockSpec((1, tn), lambda i, j: (0, j)),
            pl.BlockSpec((d, LANES), lambda i, j: (0, 0)),
            pl.BlockSpec((1, LANES), lambda i, j: (0, 0)),
        ],
        out_specs=(pl.BlockSpec((tm, tn), lambda i, j: (i, j)),
                   pl.BlockSpec((tm, LANES), lambda i, j: (i, 0))),
        scratch_shapes=[pltpu.VMEM((tm, d), BF16)],
        compiler_params=_params("parallel", "arbitrary"),
        name="inproj",
    )(x2, g, w, b, wg, bg)


def _mlstm_kernel(qk_ref, v_ref, o_ref, gt_ref, cw_ref, cb_ref, hg_ref, y_ref,
                  ext_scr, c_scr, m_scr):
    L = M_CHUNK

    @pl.when(pl.program_id(1) == 0)
    def _():
        ext_scr[0:SUBLANES, :] = jnp.zeros((SUBLANES, 2 * M_QK), F32)
        c_scr[...] = jnp.zeros(c_scr.shape, F32)
        m_scr[...] = jnp.zeros(m_scr.shape, F32)

    cur = qk_ref[...].astype(F32)
    ext_scr[SUBLANES:SUBLANES + L, :] = cur
    acc = cb_ref[...] + cw_ref[CONV_W - 1:CONV_W, :] * cur
    for j in range(CONV_W - 1):
        off = SUBLANES - (CONV_W - 1) + j
        acc = acc + cw_ref[j:j + 1, :] * ext_scr[off:off + L, :]
    ext_scr[0:SUBLANES, :] = cur[L - SUBLANES:L, :]
    qk = acc * _sigmoid(acc)

    gt = gt_ref[...]
    lf = jnp.minimum(gt, 0.0) - jnp.log(1.0 + jnp.exp(-jnp.abs(gt)))
    tri = _tril(L)
    bcols = jnp.dot(tri.astype(F32), lf, precision=HIGHEST, preferred_element_type=F32)
    gt_t = gt.T
    b_t = bcols.T
    ones_ext = jnp.ones((L, LANES), BF16)

    for h in range(M_HEADS):
        icol = gt[:, h:h + 1]
        bcol = bcols[:, M_HEADS + h:M_HEADS + h + 1]
        irow = gt_t[h:h + 1, :]
        brow = b_t[M_HEADS + h:M_HEADS + h + 1, :]
        m_prev = m_scr[h, 0:1, 0:1]
        b_last = bcol[L - 1:L, :]

        dmat = jnp.where(tri, bcol - brow + irow, -jnp.inf)
        inter = bcol + m_prev
        m_t = jnp.maximum(inter, jnp.max(dmat, axis=-1, keepdims=True))
        dexp = jnp.exp(dmat - m_t)
        inter_w = jnp.exp(inter - m_t)

        q = (qk[:, h * M_QK_DIM:(h + 1) * M_QK_DIM] * (M_QK_DIM ** -0.5)).astype(BF16)
        kf = qk[:, M_QK + h * M_QK_DIM:M_QK + (h + 1) * M_QK_DIM]
        v_ext = jnp.concatenate([v_ref[:, h * M_V_DIM:(h + 1) * M_V_DIM], ones_ext], axis=1)

        scores = (_dot_nt(q, kf.astype(BF16)) * dexp).astype(BF16)
        c_old = c_scr[h]
        nd = _dot(scores, v_ext) + inter_w * _dot(q, c_old.astype(BF16))
        den = nd[:, M_V_DIM:]
        inv = 1.0 / jnp.maximum(jnp.abs(den), jnp.exp(-m_t))
        hs = [nd[:, j * LANES:(j + 1) * LANES] * inv for j in range(M_V_DIM // LANES)]
        hh = jnp.concatenate(hs, axis=1)

        m_new = m_t[L - 1:L, :]
        w_s = jnp.exp(b_last - bcol + icol - m_new)
        decay = jnp.exp(b_last + m_prev - m_new)
        kw = (kf * w_s).astype(BF16)
        c_scr[h] = decay * c_old + _dot_tn(kw, v_ext)
        m_scr[h] = jnp.broadcast_to(m_new, (SUBLANES, LANES))

        hn = hh * lax.rsqrt(jnp.mean(hh * hh, axis=-1, keepdims=True) + EPS)
        sl = slice(h * M_V_DIM, (h + 1) * M_V_DIM)
        og = _sigmoid(o_ref[:, sl].astype(F32))
        y_ref[:, sl] = (og * hn * hg_ref[:, sl]).astype(BF16)


def _mlstm(proj, gates, conv_w, conv_b, head_g, nb, nc):
    t = proj.shape[0]
    L = M_CHUNK
    row = lambda b, c: b * nc + c
    return pl.pallas_call(
        _mlstm_kernel,
        out_shape=jax.ShapeDtypeStruct((t, M_V), BF16),
        grid=(nb, nc),
        in_specs=[
            pl.BlockSpec((L, 2 * M_QK), lambda b, c: (row(b, c), 0)),
            pl.BlockSpec((L, M_V), lambda b, c: (row(b, c), 1)),
            pl.BlockSpec((L, M_V), lambda b, c: (row(b, c), 2)),
            pl.BlockSpec((L, LANES), lambda b, c: (row(b, c), 0)),
            pl.BlockSpec((CONV_W, 2 * M_QK), lambda b, c: (0, 0)),
            pl.BlockSpec((1, 2 * M_QK), lambda b, c: (0, 0)),
            pl.BlockSpec((1, M_V), lambda b, c: (0, 0)),
        ],
        out_specs=pl.BlockSpec((L, M_V), lambda b, c: (row(b, c), 0)),
        scratch_shapes=[
            pltpu.VMEM((SUBLANES + L, 2 * M_QK), F32),
            pltpu.VMEM((M_HEADS, M_QK_DIM, M_V_DIM + LANES), F32),
            pltpu.VMEM((M_HEADS, SUBLANES, LANES), F32),
        ],
        compiler_params=_params("parallel", "arbitrary"),
        name="mlstm",
    )(proj, proj, proj, gates, conv_w, conv_b, head_g)


def _hgrn_kernel(gf_ref, gi_ref, gq_ref, gg_ref, lb_ref, hg_ref, y_ref, st_scr):
    L, SUB, NSUB = G_CHUNK, G_SUB, G_CHUNK // G_SUB

    @pl.when(pl.program_id(1) == 0)
    def _():
        st_scr[...] = jnp.zeros(st_scr.shape, F32)

    lb = lb_ref[...]
    f = lb + (1.0 - lb) * _sigmoid(gf_ref[...].astype(F32))
    k_all = 1.0 - f
    cum_all = jnp.dot(_tril(L).astype(F32), jnp.log(f), precision=HIGHEST,
                      preferred_element_type=F32)
    gq = gq_ref[...].astype(F32)
    q_all = gq * _sigmoid(gq)

    row = lax.broadcasted_iota(jnp.int32, (L, G_DIM), 0)
    row_sub = lax.broadcasted_iota(jnp.int32, (SUB, G_DIM), 0)
    ones_sq = jnp.ones((G_DIM, G_DIM), BF16)

    for h in range(G_HEADS):
        sl = slice(h * G_DIM, (h + 1) * G_DIM)
        cum, q, k = cum_all[:, sl], q_all[:, sl], k_all[:, sl]
        iv = gi_ref[:, sl]
        ivf = iv.astype(F32)
        ends = [cum[SUB * b + SUB - 1:SUB * b + SUB, :] for b in range(NSUB)]
        last = ends[NSUB - 1]
        end_of_row = jnp.concatenate([jnp.broadcast_to(e, (SUB, G_DIM)) for e in ends], axis=0)

        khat = k * jnp.exp(end_of_row - cum)
        qs, ks = [], []
        for b in range(NSUB - 1):
            lo = SUB * (b + 1)
            qs.append(jnp.where(row >= lo, q * jnp.exp(jnp.minimum(cum - ends[b], 0.0)), 0.0))
            ks.append(jnp.where((row >= lo - SUB) & (row < lo), khat, 0.0))
        att = _dot_nt(jnp.concatenate(qs, axis=1).astype(BF16),
                      jnp.concatenate(ks, axis=1).astype(BF16))
        o = _dot(att.astype(BF16), iv)

        st = st_scr[h]
        o = o + _dot_nt((q * jnp.exp(cum)).astype(BF16), st.astype(BF16))

        o_diag = []
        H = SUBLANES
        for a in range(NSUB):
            r0 = SUB * a
            q_a, k_a, c_a, i_a = (z[r0:r0 + SUB, :] for z in (q, k, cum, ivf))
            ms = []
            for s in range(SUB):
                lo = 0 if s < H else H
                diff = c_a[lo:, :] - c_a[s:s + 1, :]
                part = jnp.where(row_sub[lo:lo + H, :] >= s, diff[0:H, :], -jnp.inf)
                diff = jnp.concatenate([part, diff[H:, :]], axis=0) if lo == 0 else part
                ms.append(q_a[lo:, :] * (k_a[s:s + 1, :] * jnp.exp(diff)))
            red = _dot(jnp.concatenate(ms, axis=0).astype(BF16), ones_sq)
            acc_lo = red[0:H, :] * i_a[0:1, :]
            acc_hi = red[H:SUB, :] * i_a[0:1, :]
            for s in range(1, H):
                acc_lo = acc_lo + red[s * SUB:s * SUB + H, :] * i_a[s:s + 1, :]
                acc_hi = acc_hi + red[s * SUB + H:(s + 1) * SUB, :] * i_a[s:s + 1, :]
            for s in range(H, SUB):
                off = H * SUB + (s - H) * H
                acc_hi = acc_hi + red[off:off + H, :] * i_a[s:s + 1, :]
            o_diag += [acc_lo, acc_hi]
        o = o + jnp.concatenate(o_diag, axis=0)

        kd = (k * jnp.exp(last - cum)).astype(BF16)
        st_scr[h] = jnp.exp(last) * st + _dot_tn(iv, kd)

        hn = o * lax.rsqrt(jnp.mean(o * o, axis=-1, keepdims=True) + EPS)
        gg = gg_ref[:, sl].astype(F32)
        y_ref[:, sl] = (hn * hg_ref[:, sl] * (gg * _sigmoid(gg))).astype(BF16)


def _hgrn(proj, lb, head_g, nb, nc, col0):
    t = proj.shape[0]
    L = G_CHUNK
    row = lambda b, c: b * nc + c
    spec = lambda j: pl.BlockSpec((L, G_W), lambda b, c: (row(b, c), col0 + j))
    return pl.pallas_call(
        _hgrn_kernel,
        out_shape=jax.ShapeDtypeStruct((t, G_W), BF16),
        grid=(nb, nc),
        in_specs=[spec(0), spec(1), spec(2), spec(3),
                  pl.BlockSpec((1, G_W), lambda b, c: (0, 0)),
                  pl.BlockSpec((1, G_W), lambda b, c: (0, 0))],
        out_specs=pl.BlockSpec((L, G_W), lambda b, c: (row(b, c), 0)),
        scratch_shapes=[pltpu.VMEM((G_HEADS, G_DIM, G_DIM), F32)],
        compiler_params=_params("parallel", "arbitrary"),
        name="hgrn2",
    )(proj, proj, proj, proj, lb, head_g)


def _merge_kernel(x_ref, ym_ref, yg_ref, am_ref, ag_ref, wm_ref, wg_ref, wo_ref, g2_ref,
                  x1_ref, u2_ref):
    zm = _dot(ym_ref[...], wm_ref[...])
    zg = _dot(yg_ref[...], wg_ref[...])
    z = _sigmoid(am_ref[...].astype(F32)) * zm + _sigmoid(ag_ref[...].astype(F32)) * zg
    x1 = x_ref[...] + _dot(z.astype(BF16), wo_ref[...])
    x1_ref[...] = x1
    u2 = x1 * lax.rsqrt(jnp.mean(x1 * x1, axis=-1, keepdims=True) + EPS) * g2_ref[...]
    u2_ref[...] = u2.astype(BF16)


def _merge(x2, ym, yg, proj, wm, wg, wo, g2, tm, col_am):
    t, d = x2.shape
    rows = lambda w: pl.BlockSpec((tm, w), lambda i: (i, 0))
    full = lambda a: pl.BlockSpec(a.shape, lambda i: (0, 0))
    return pl.pallas_call(
        _merge_kernel,
        out_shape=(jax.ShapeDtypeStruct((t, d), F32), jax.ShapeDtypeStruct((t, d), BF16)),
        grid=(t // tm,),
        in_specs=[rows(d), rows(M_V), rows(G_W),
                  pl.BlockSpec((tm, d), lambda i: (i, col_am)),
                  pl.BlockSpec((tm, d), lambda i: (i, col_am + 1)),
                  full(wm), full(wg), full(wo), full(g2)],
        out_specs=(rows(d), rows(d)),
        compiler_params=_params("parallel"),
        name="merge",
    )(x2, ym, yg, proj, proj, wm, wg, wo, g2)


def _sort16_pairs():
    def merge(lo, hi, r):
        step = r * 2
        if step < hi - lo:
            yield from merge(lo, hi, step)
            yield from merge(lo + r, hi, step)
            yield from [(i, i + r) for i in range(lo + r, hi - r, step)]
        else:
            yield (lo, lo + r)

    def sort(lo, hi):
        if hi - lo >= 1:
            mid = lo + (hi - lo) // 2
            yield from sort(lo, mid)
            yield from sort(mid + 1, hi)
            yield from merge(lo, hi, 1)

    return list(sort(0, P_TOPK - 1))


def _top_rows(vals, k):
    out = []
    for _ in range(k):
        m = jnp.max(vals, axis=0, keepdims=True)
        out.append(m)
        vals = jnp.where(vals == m, -jnp.inf, vals)
    return out


def _top16_of_128(vals):
    K = P_TOPK
    v = [vals[SUBLANES * g:SUBLANES * (g + 1), :] for g in range(K)]
    for a, b in _sort16_pairs():
        v[a], v[b] = jnp.maximum(v[a], v[b]), jnp.minimum(v[a], v[b])
    out = []
    for r in range(K):
        m = jnp.max(v[0], axis=0, keepdims=True)
        out.append(m)
        if r == K - 1:
            break
        hit = v[0] == m
        depth = K - 1 - r
        for i in range(depth):
            v[i] = jnp.where(hit, v[i + 1], v[i])
    return out


def _route_kernel(u2_ref, wq_ref, keys_ref, thr_ref, w0_ref, s1_ref, e1_ref, cand_scr):
    K = P_TOPK
    tb = u2_ref.shape[0]
    q_t = _dot_nt(wq_ref[...], u2_ref[...])
    pairs = [(r, c) for r in range(K) for c in range(K) if (r + 1) * (c + 1) <= K]
    n_pad = cand_scr.shape[0] - len(pairs)
    cand_scr[len(pairs):, :] = jnp.full((n_pad, tb), -jnp.inf, F32)

    for h in range(P_HEADS):
        s = []
        for p in range(2):
            r0 = (2 * h + p) * P_NKEYS
            s.append(_dot(keys_ref[2 * h + p], q_t[r0:r0 + P_NKEYS, :].astype(BF16)))
        a = _top16_of_128(s[0])
        b = _top16_of_128(s[1])
        for n, (r, c) in enumerate(pairs):
            cand_scr[n:n + 1, :] = a[r] + b[c]
        top = _top_rows(cand_scr[...], K)
        z = jnp.ones_like(top[0])
        for r in range(1, K):
            z = z + jnp.exp(top[r] - top[0])
        thr_ref[h] = top[K - 1] - s[0]
        w0_ref[h] = 0.5 * jnp.exp(s[0] - a[0]) / z
        s1_ref[h] = s[1]
        e1_ref[h] = jnp.exp(s[1] - b[0])


def _route(u2, wq_t, keys, tb):
    t, d = u2.shape
    n_pairs = sum(1 for r in range(P_TOPK) for c in range(P_TOPK) if (r + 1) * (c + 1) <= P_TOPK)
    n_cand = -(-n_pairs // SUBLANES) * SUBLANES
    out = jax.ShapeDtypeStruct((P_HEADS, P_NKEYS, t), F32)
    ospec = pl.BlockSpec((P_HEADS, P_NKEYS, tb), lambda i: (0, 0, i))
    return pl.pallas_call(
        _route_kernel,
        out_shape=(out, out, out, out),
        grid=(t // tb,),
        in_specs=[pl.BlockSpec((tb, d), lambda i: (i, 0)),
                  pl.BlockSpec(wq_t.shape, lambda i: (0, 0)),
                  pl.BlockSpec(keys.shape, lambda i: (0, 0, 0))],
        out_specs=(ospec, ospec, ospec, ospec),
        scratch_shapes=[pltpu.VMEM((n_cand, tb), F32)],
        compiler_params=_params("parallel"),
        name="route",
    )(u2, wq_t, keys)


PEER_JB = 32
PEER_IB = 4


def _peer_step(u2_ref, u_ref, vt_ref, thr_ref, w0_ref, s1_ref, e1_ref, acc_scr,
               h_in, h_out, p_in, p_out):
    eb, d = u_ref.shape
    tb = u2_ref.shape[0]
    n_grp = eb // (PEER_IB * P_NKEYS)
    for ib in range(n_grp):
        er = slice(ib * (eb // n_grp), (ib + 1) * (eb // n_grp))
        dr = slice(ib * (d // n_grp), (ib + 1) * (d // n_grp))
        h_out[er, :] = _dot_nt(u_ref[er, :], u2_ref[...])
        acc_scr[dr, :] += _dot(vt_ref[dr, :], p_in[...])
        for tc in range(tb // LANES):
            lanes = slice(tc * LANES, (tc + 1) * LANES)
            for jb in range(P_NKEYS // PEER_JB):
                jrows = slice(jb * PEER_JB, (jb + 1) * PEER_JB)
                gate = [None] * PEER_IB
                for h in range(P_HEADS):
                    s1 = s1_ref[h, jrows, lanes]
                    e1 = e1_ref[h, jrows, lanes]
                    for k in range(PEER_IB):
                        i = ib * PEER_IB + k
                        term = (jnp.where(s1 >= thr_ref[h, i:i + 1, lanes], e1, 0.0)
                                * w0_ref[h, i:i + 1, lanes])
                        gate[k] = term if gate[k] is None else gate[k] + term
                for k in range(PEER_IB):
                    r0 = (ib * PEER_IB + k) * P_NKEYS + jb * PEER_JB
                    hv = h_in[r0:r0 + PEER_JB, lanes]
                    act = hv * (1.0 + lax.erf(hv * (2.0 ** -0.5)))
                    p_out[r0:r0 + PEER_JB, lanes] = (act * gate[k]).astype(BF16)


def _peer_kernel(n_e, u2_ref, x1_ref, fg_ref, u_ref, vt_ref, thr_ref, w0_ref, s1_ref, e1_ref,
                 out_ref, acc_scr, h_scr, p_scr):
    f = pl.program_id(0)

    @pl.when(f == 0)
    def _():
        acc_scr[...] = jnp.zeros(acc_scr.shape, F32)
        h_scr[1] = jnp.zeros(h_scr.shape[1:], F32)
        p_scr[1] = jnp.zeros(p_scr.shape[1:], BF16)

    step = functools.partial(_peer_step, u2_ref, u_ref, vt_ref, thr_ref, w0_ref, s1_ref, e1_ref,
                             acc_scr)

    @pl.when(f % 2 == 0)
    def _():
        step(h_scr.at[1], h_scr.at[0], p_scr.at[1], p_scr.at[0])

    @pl.when(f % 2 == 1)
    def _():
        step(h_scr.at[0], h_scr.at[1], p_scr.at[0], p_scr.at[1])

    @pl.when((f >= 2) & ((f - 2) % n_e == n_e - 1))
    def _():
        x2 = x1_ref[...] + acc_scr[...].T
        out_ref[...] = (x2 * lax.rsqrt(jnp.mean(x2 * x2, axis=-1, keepdims=True) + EPS)
                        * fg_ref[...])
        acc_scr[...] = jnp.zeros(acc_scr.shape, F32)


def _peer(u2, x1, fg, u_emb, v_t, thr, w0, s1, e1, tb, eb):
    t, d = u2.shape
    n_e = u_emb.shape[0] // eb
    n_i = eb // P_NKEYS
    n = (t // tb) * n_e
    blk = lambda f, lag: jnp.clip(f - lag, 0, n - 1)
    tok = lambda lag: (lambda f: (blk(f, lag) // n_e, 0))
    gate_blk = lambda f: (0, blk(f, 1) % n_e, blk(f, 1) // n_e)
    key_blk = lambda f: (0, 0, blk(f, 1) // n_e)
    return pl.pallas_call(
        functools.partial(_peer_kernel, n_e),
        out_shape=jax.ShapeDtypeStruct((t, d), F32),
        grid=(n + 2,),
        in_specs=[pl.BlockSpec((tb, d), tok(0)),
                  pl.BlockSpec((tb, d), tok(2)),
                  pl.BlockSpec((1, d), lambda f: (0, 0)),
                  pl.BlockSpec((eb, d), lambda f: (blk(f, 0) % n_e, 0)),
                  pl.BlockSpec((d, eb), lambda f: (0, blk(f, 2) % n_e)),
                  pl.BlockSpec((P_HEADS, n_i, tb), gate_blk),
                  pl.BlockSpec((P_HEADS, n_i, tb), gate_blk),
                  pl.BlockSpec((P_HEADS, P_NKEYS, tb), key_blk),
                  pl.BlockSpec((P_HEADS, P_NKEYS, tb), key_blk)],
        out_specs=pl.BlockSpec((tb, d), tok(2)),
        scratch_shapes=[pltpu.VMEM((d, tb), F32), pltpu.VMEM((2, eb, tb), F32),
                        pltpu.VMEM((2, eb, tb), BF16)],
        compiler_params=_params("arbitrary"),
        name="peer",
    )(u2, x1, fg, u_emb, v_t, thr, w0, s1, e1)


def _tiles(t):
    pick = lambda pref: next(c for c in (pref, 512, 256, 128) if c <= pref and t % c == 0)
    return dict(inproj_tm=pick(1024), inproj_tn=1024, merge_tm=pick(512),
                route_tb=pick(256), peer_tb=pick(512), peer_eb=1024)


def kernel(x, norm1_g, w_in, b_in, conv_w, conv_b, m_head_g, lb_table, g_head_g, w_br_m, w_br_g,
           w_out, norm2_g, w_pq, sub_keys, u_emb, v_emb, final_g):
    nb, seq, d = x.shape
    t = nb * seq
    depth = w_in.shape[0]
    assert d == 1024 and seq % M_CHUNK == 0 and t % LANES == 0
    tl = _tiles(t)
    n_gate = 2 * M_HEADS
    g0 = 2 * M_QK + 2 * M_V
    assert (w_in.shape[2] - n_gate) % tl["inproj_tn"] == 0

    lb_all = jnp.cumsum(jax.nn.softmax(lb_table.astype(F32), axis=0), axis=0)
    x2 = x.reshape(t, d)
    row = lambda a: a.reshape(1, -1).astype(F32)

    for l in range(depth):
        w_main = jnp.concatenate([w_in[l][:, :g0], w_in[l][:, g0 + n_gate:]], axis=1).astype(BF16)
        b_main = jnp.concatenate([b_in[l][:g0], b_in[l][g0 + n_gate:]]).reshape(1, -1)
        w_gate = jnp.pad(w_in[l][:, g0:g0 + n_gate], ((0, 0), (0, LANES - n_gate))).astype(BF16)
        b_gate = jnp.pad(b_in[l][g0:g0 + n_gate], (0, LANES - n_gate)).reshape(1, -1)

        proj, gates = _inproj(x2, row(norm1_g[l]), w_main, b_main, w_gate, b_gate,
                              tl["inproj_tm"], tl["inproj_tn"])
        y_m = _mlstm(proj, gates, conv_w[l], row(conv_b[l]), row(m_head_g[l]),
                     nb, seq // M_CHUNK)
        y_g = _hgrn(proj, row(lb_all[l]), row(g_head_g[l]), nb, seq // G_CHUNK, g0 // G_W)
        x1, u2 = _merge(x2, y_m, y_g, proj, w_br_m[l].astype(BF16), w_br_g[l].astype(BF16),
                        w_out[l].astype(BF16), row(norm2_g[l]), tl["merge_tm"],
                        (g0 + 4 * G_W) // d)

        keys = sub_keys[l].reshape(2 * P_HEADS, P_NKEYS, -1).astype(BF16)
        thr, w0, s1, e1 = _route(u2, w_pq[l].T.astype(BF16), keys, tl["route_tb"])
        assert depth == 1
        x2 = _peer(u2, x1, row(final_g), u_emb[l].astype(BF16), v_emb[l].T.astype(BF16),
                   thr, w0, s1, e1, tl["peer_tb"], tl["peer_eb"])
    return x2.reshape(nb, seq, d)
```

```python
import functools

import jax
import jax.numpy as jnp
from jax import lax
from jax.experimental import pallas as pl
from jax.experimental.pallas import tpu as pltpu

F32 = jnp.float32
BF16 = jnp.bfloat16
HIGHEST = lax.Precision.HIGHEST

EPS = 1e-6
LOG2E = 1.4426950408889634
LANES = 128
SUBLANES = 8
VMEM_LIMIT = 48 * 1024 * 1024

M_HEADS, M_QK_DIM, M_V_DIM, M_CHUNK, CONV_W = 4, 256, 512, 128, 4
G_HEADS, G_DIM, G_CHUNK, G_SUB = 8, 128, 64, 16
P_HEADS, P_NKEYS, P_TOPK = 8, 128, 16
M_QK = M_HEADS * M_QK_DIM
M_V = M_HEADS * M_V_DIM
G_W = G_HEADS * G_DIM


def _params(*sem):
    return pltpu.CompilerParams(dimension_semantics=sem, vmem_limit_bytes=VMEM_LIMIT)


def _dot(a, b):
    return jnp.dot(a, b, preferred_element_type=F32)


def _dot_nt(a, b):
    return lax.dot_general(a, b, (((1,), (1,)), ((), ())), preferred_element_type=F32)


def _dot_tn(a, b):
    return lax.dot_general(a, b, (((0,), (0,)), ((), ())), preferred_element_type=F32)


def _sigmoid(x):
    return 1.0 / (1.0 + jnp.exp2(x * (-LOG2E)))


def _tril(n):
    r = lax.broadcasted_iota(jnp.int32, (n, n), 0)
    c = lax.broadcasted_iota(jnp.int32, (n, n), 1)
    return (r >= c)


def _inproj_kernel(x_ref, g_ref, w_ref, b_ref, wg_ref, bg_ref, out_ref, gate_ref, u_scr):
    @pl.when(pl.program_id(1) == 0)
    def _():
        x = x_ref[...]
        u = x * lax.rsqrt(jnp.mean(x * x, axis=-1, keepdims=True) + EPS) * g_ref[...]
        u_scr[...] = u.astype(BF16)
        gate_ref[...] = _dot(u.astype(BF16), wg_ref[...]) + bg_ref[...]

    out_ref[...] = (_dot(u_scr[...], w_ref[...]) + b_ref[...]).astype(BF16)


def _inproj(x2, g, w, b, wg, bg, tm, tn):
    t, d = x2.shape
    n = w.shape[1]
    return pl.pallas_call(
        _inproj_kernel,
        out_shape=(jax.ShapeDtypeStruct((t, n), BF16), jax.ShapeDtypeStruct((t, LANES), F32)),
        grid=(t // tm, n // tn),
        in_specs=[
            pl.BlockSpec((tm, d), lambda i, j: (i, 0)),
            pl.BlockSpec((1, d), lambda i, j: (0, 0)),
            pl.BlockSpec((d, tn), lambda i, j: (0, j)),
            pl.BlockSpec((1, tn), lambda i, j: (0, j)),
            pl.BlockSpec((d, LANES), lambda i, j: (0, 0)),
            pl.BlockSpec((1, LANES), lambda i, j: (0, 0)),
        ],
        out_specs=(pl.BlockSpec((tm, tn), lambda i, j: (i, j)),
                   pl.BlockSpec((tm, LANES), lambda i, j: (i, 0))),
        scratch_shapes=[pltpu.VMEM((tm, d), BF16)],
        compiler_params=_params("parallel", "arbitrary"),
        name="inproj",
    )(x2, g, w, b, wg, bg)


def _mlstm_kernel(qk_ref, v_ref, o_ref, gt_ref, cw_ref, cb_ref, hg_ref, y_ref,
                  ext_scr, c_scr, m_scr):
    L = M_CHUNK

    @pl.when(pl.program_id(1) == 0)
    def _():
        ext_scr[0:SUBLANES, :] = jnp.zeros((SUBLANES, 2 * M_QK), F32)
        c_scr[...] = jnp.zeros(c_scr.shape, F32)
        m_scr[...] = jnp.zeros(m_scr.shape, F32)

    cur = qk_ref[...].astype(F32)
    ext_scr[SUBLANES:SUBLANES + L, :] = cur
    acc = cb_ref[...] + cw_ref[CONV_W - 1:CONV_W, :] * cur
    for j in range(CONV_W - 1):
        off = SUBLANES - (CONV_W - 1) + j
        acc = acc + cw_ref[j:j + 1, :] * ext_scr[off:off + L, :]
    ext_scr[0:SUBLANES, :] = cur[L - SUBLANES:L, :]
    qk = acc * _sigmoid(acc)

    gt = gt_ref[...]
    lf = jnp.minimum(gt, 0.0) - jnp.log(1.0 + jnp.exp(-jnp.abs(gt)))
    tri = _tril(L)
    bcols = jnp.dot(tri.astype(F32), lf, precision=HIGHEST, preferred_element_type=F32)
    gt_t = gt.T
    b_t = bcols.T
    ones_ext = jnp.ones((L, LANES), BF16)

    for h in range(M_HEADS):
        icol = gt[:, h:h + 1]
        bcol = bcols[:, M_HEADS + h:M_HEADS + h + 1]
        irow = gt_t[h:h + 1, :]
        brow = b_t[M_HEADS + h:M_HEADS + h + 1, :]
        m_prev = m_scr[h, 0:1, 0:1]
        b_last = bcol[L - 1:L, :]

        dmat = jnp.where(tri, bcol - brow + irow, -jnp.inf)
        inter = bcol + m_prev
        m_t = jnp.maximum(inter, jnp.max(dmat, axis=-1, keepdims=True))
        dexp = jnp.exp(dmat - m_t)
        inter_w = jnp.exp(inter - m_t)

        q = (qk[:, h * M_QK_DIM:(h + 1) * M_QK_DIM] * (M_QK_DIM ** -0.5)).astype(BF16)
        kf = qk[:, M_QK + h * M_QK_DIM:M_QK + (h + 1) * M_QK_DIM]
        v_ext = jnp.concatenate([v_ref[:, h * M_V_DIM:(h + 1) * M_V_DIM], ones_ext], axis=1)

        scores = (_dot_nt(q, kf.astype(BF16)) * dexp).astype(BF16)
        c_old = c_scr[h]
        nd = _dot(scores, v_ext) + inter_w * _dot(q, c_old.astype(BF16))
        den = nd[:, M_V_DIM:]
        inv = 1.0 / jnp.maximum(jnp.abs(den), jnp.exp(-m_t))
        hs = [nd[:, j * LANES:(j + 1) * LANES] * inv for j in range(M_V_DIM // LANES)]
        hh = jnp.concatenate(hs, axis=1)

        m_new = m_t[L - 1:L, :]
        w_s = jnp.exp(b_last - bcol + icol - m_new)
        decay = jnp.exp(b_last + m_prev - m_new)
        kw = (kf * w_s).astype(BF16)
        c_scr[h] = decay * c_old + _dot_tn(kw, v_ext)
        m_scr[h] = jnp.broadcast_to(m_new, (SUBLANES, LANES))

        hn = hh * lax.rsqrt(jnp.mean(hh * hh, axis=-1, keepdims=True) + EPS)
        sl = slice(h * M_V_DIM, (h + 1) * M_V_DIM)
        og = _sigmoid(o_ref[:, sl].astype(F32))
        y_ref[:, sl] = (og * hn * hg_ref[:, sl]).astype(BF16)


def _mlstm(proj, gates, conv_w, conv_b, head_g, nb, nc):
    t = proj.shape[0]
    L = M_CHUNK
    row = lambda b, c: b * nc + c
    return pl.pallas_call(
        _mlstm_kernel,
        out_shape=jax.ShapeDtypeStruct((t, M_V), BF16),
        grid=(nb, nc),
        in_specs=[
            pl.BlockSpec((L, 2 * M_QK), lambda b, c: (row(b, c), 0)),
            pl.BlockSpec((L, M_V), lambda b, c: (row(b, c), 1)),
            pl.BlockSpec((L, M_V), lambda b, c: (row(b, c), 2)),
            pl.BlockSpec((L, LANES), lambda b, c: (row(b, c), 0)),
            pl.BlockSpec((CONV_W, 2 * M_QK), lambda b, c: (0, 0)),
            pl.BlockSpec((1, 2 * M_QK), lambda b, c: (0, 0)),
            pl.BlockSpec((1, M_V), lambda b, c: (0, 0)),
        ],
        out_specs=pl.BlockSpec((L, M_V), lambda b, c: (row(b, c), 0)),
        scratch_shapes=[
            pltpu.VMEM((SUBLANES + L, 2 * M_QK), F32),
            pltpu.VMEM((M_HEADS, M_QK_DIM, M_V_DIM + LANES), F32),
            pltpu.VMEM((M_HEADS, SUBLANES, LANES), F32),
        ],
        compiler_params=_params("parallel", "arbitrary"),
        name="mlstm",
    )(proj, proj, proj, gates, conv_w, conv_b, head_g)


def _hgrn_kernel(gf_ref, gi_ref, gq_ref, gg_ref, lb_ref, hg_ref, y_ref, st_scr):
    L, SUB, NSUB = G_CHUNK, G_SUB, G_CHUNK // G_SUB

    @pl.when(pl.program_id(1) == 0)
    def _():
        st_scr[...] = jnp.zeros(st_scr.shape, F32)

    lb = lb_ref[...]
    f = lb + (1.0 - lb) * _sigmoid(gf_ref[...].astype(F32))
    k_all = 1.0 - f
    cum_all = jnp.dot(_tril(L).astype(F32), jnp.log(f), precision=HIGHEST,
                      preferred_element_type=F32)
    gq = gq_ref[...].astype(F32)
    q_all = gq * _sigmoid(gq)

    row = lax.broadcasted_iota(jnp.int32, (L, G_DIM), 0)
    row_sub = lax.broadcasted_iota(jnp.int32, (SUB, G_DIM), 0)
    ones_sq = jnp.ones((G_DIM, G_DIM), BF16)

    for h in range(G_HEADS):
        sl = slice(h * G_DIM, (h + 1) * G_DIM)
        cum, q, k = cum_all[:, sl], q_all[:, sl], k_all[:, sl]
        iv = gi_ref[:, sl]
        ivf = iv.astype(F32)
        ends = [cum[SUB * b + SUB - 1:SUB * b + SUB, :] for b in range(NSUB)]
        last = ends[NSUB - 1]
        end_of_row = jnp.concatenate([jnp.broadcast_to(e, (SUB, G_DIM)) for e in ends], axis=0)

        khat = k * jnp.exp(end_of_row - cum)
        qs, ks = [], []
        for b in range(NSUB - 1):
            lo = SUB * (b + 1)
            qs.append(jnp.where(row >= lo, q * jnp.exp(jnp.minimum(cum - ends[b], 0.0)), 0.0))
            ks.append(jnp.where((row >= lo - SUB) & (row < lo), khat, 0.0))
        att = _dot_nt(jnp.concatenate(qs, axis=1).astype(BF16),
                      jnp.concatenate(ks, axis=1).astype(BF16))
        o = _dot(att.astype(BF16), iv)

        st = st_scr[h]
        o = o + _dot_nt((q * jnp.exp(cum)).astype(BF16), st.astype(BF16))

        o_diag = []
        H = SUBLANES
        for a in range(NSUB):
            r0 = SUB * a
            q_a, k_a, c_a, i_a = (z[r0:r0 + SUB, :] for z in (q, k, cum, ivf))
            ms = []
            for s in range(SUB):
                lo = 0 if s < H else H
                diff = c_a[lo:, :] - c_a[s:s + 1, :]
                part = jnp.where(row_sub[lo:lo + H, :] >= s, diff[0:H, :], -jnp.inf)
                diff = jnp.concatenate([part, diff[H:, :]], axis=0) if lo == 0 else part
                ms.append(q_a[lo:, :] * (k_a[s:s + 1, :] * jnp.exp(diff)))
            red = _dot(jnp.concatenate(ms, axis=0).astype(BF16), ones_sq)
            acc_lo = red[0:H, :] * i_a[0:1, :]
            acc_hi = red[H:SUB, :] * i_a[0:1, :]
            for s in range(1, H):
                acc_lo = acc_lo + red[s * SUB:s * SUB + H, :] * i_a[s:s + 1, :]
                acc_hi = acc_hi + red[s * SUB + H:(s + 1) * SUB, :] * i_a[s:s + 1, :]
            for s in range(H, SUB):
                off = H * SUB + (s - H) * H
                acc_hi = acc_hi + red[off:off + H, :] * i_a[s:s + 1, :]
            o_diag += [acc_lo, acc_hi]
        o = o + jnp.concatenate(o_diag, axis=0)

        kd = (k * jnp.exp(last - cum)).astype(BF16)
        st_scr[h] = jnp.exp(last) * st + _dot_tn(iv, kd)

        hn = o * lax.rsqrt(jnp.mean(o * o, axis=-1, keepdims=True) + EPS)
        gg = gg_ref[:, sl].astype(F32)
        y_ref[:, sl] = (hn * hg_ref[:, sl] * (gg * _sigmoid(gg))).astype(BF16)


def _hgrn(proj, lb, head_g, nb, nc, col0):
    t = proj.shape[0]
    L = G_CHUNK
    row = lambda b, c: b * nc + c
    spec = lambda j: pl.BlockSpec((L, G_W), lambda b, c: (row(b, c), col0 + j))
    return pl.pallas_call(
        _hgrn_kernel,
        out_shape=jax.ShapeDtypeStruct((t, G_W), BF16),
        grid=(nb, nc),
        in_specs=[spec(0), spec(1), spec(2), spec(3),
                  pl.BlockSpec((1, G_W), lambda b, c: (0, 0)),
                  pl.BlockSpec((1, G_W), lambda b, c: (0, 0))],
        out_specs=pl.BlockSpec((L, G_W), lambda b, c: (row(b, c), 0)),
        scratch_shapes=[pltpu.VMEM((G_HEADS, G_DIM, G_DIM), F32)],
        compiler_params=_params("parallel", "arbitrary"),
        name="hgrn2",
    )(proj, proj, proj, proj, lb, head_g)


def _merge_kernel(x_ref, ym_ref, yg_ref, am_ref, ag_ref, wm_ref, wg_ref, wo_ref, g2_ref,
                  x1_ref, u2_ref):
    zm = _dot(ym_ref[...], wm_ref[...])
    zg = _dot(yg_ref[...], wg_ref[...])
    z = _sigmoid(am_ref[...].astype(F32)) * zm + _sigmoid(ag_ref[...].astype(F32)) * zg
    x1 = x_ref[...] + _dot(z.astype(BF16), wo_ref[...])
    x1_ref[...] = x1
    u2 = x1 * lax.rsqrt(jnp.mean(x1 * x1, axis=-1, keepdims=True) + EPS) * g2_ref[...]
    u2_ref[...] = u2.astype(BF16)


def _merge(x2, ym, yg, proj, wm, wg, wo, g2, tm, col_am):
    t, d = x2.shape
    rows = lambda w: pl.BlockSpec((tm, w), lambda i: (i, 0))
    full = lambda a: pl.BlockSpec(a.shape, lambda i: (0, 0))
    return pl.pallas_call(
        _merge_kernel,
        out_shape=(jax.ShapeDtypeStruct((t, d), F32), jax.ShapeDtypeStruct((t, d), BF16)),
        grid=(t // tm,),
        in_specs=[rows(d), rows(M_V), rows(G_W),
                  pl.BlockSpec((tm, d), lambda i: (i, col_am)),
                  pl.BlockSpec((tm, d), lambda i: (i, col_am + 1)),
                  full(wm), full(wg), full(wo), full(g2)],
        out_specs=(rows(d), rows(d)),
        compiler_params=_params("parallel"),
        name="merge",
    )(x2, ym, yg, proj, proj, wm, wg, wo, g2)


def _sort16_pairs():
    def merge(lo, hi, r):
        step = r * 2
        if step < hi - lo:
            yield from merge(lo, hi, step)
            yield from merge(lo + r, hi, step)
            yield from [(i, i + r) for i in range(lo + r, hi - r, step)]
        else:
            yield (lo, lo + r)

    def sort(lo, hi):
        if hi - lo >= 1:
            mid = lo + (hi - lo) // 2
            yield from sort(lo, mid)
            yield from sort(mid + 1, hi)
            yield from merge(lo, hi, 1)

    return list(sort(0, P_TOPK - 1))


def _top_rows(vals, k):
    out = []
    for _ in range(k):
        m = jnp.max(vals, axis=0, keepdims=True)
        out.append(m)
        vals = jnp.where(vals == m, -jnp.inf, vals)
    return out


def _top16_of_128(vals):
    K = P_TOPK
    v = [vals[SUBLANES * g:SUBLANES * (g + 1), :] for g in range(K)]
    for a, b in _sort16_pairs():
        v[a], v[b] = jnp.maximum(v[a], v[b]), jnp.minimum(v[a], v[b])
    out = []
    for r in range(K):
        m = jnp.max(v[0], axis=0, keepdims=True)
        out.append(m)
        if r == K - 1:
            break
        hit = v[0] == m
        depth = K - 1 - r
        for i in range(depth):
            v[i] = jnp.where(hit, v[i + 1], v[i])
    return out


def _route_kernel(u2_ref, wq_ref, keys_ref, thr_ref, w0_ref, s1_ref, e1_ref, cand_scr):
    K = P_TOPK
    tb = u2_ref.shape[0]
    q_t = _dot_nt(wq_ref[...], u2_ref[...])
    pairs = [(r, c) for r in range(K) for c in range(K) if (r + 1) * (c + 1) <= K]
    n_pad = cand_scr.shape[0] - len(pairs)
    cand_scr[len(pairs):, :] = jnp.full((n_pad, tb), -jnp.inf, F32)

    for h in range(P_HEADS):
        s = []
        for p in range(2):
            r0 = (2 * h + p) * P_NKEYS
            s.append(_dot(keys_ref[2 * h + p], q_t[r0:r0 + P_NKEYS, :].astype(BF16)))
        a = _top16_of_128(s[0])
        b = _top16_of_128(s[1])
        for n, (r, c) in enumerate(pairs):
            cand_scr[n:n + 1, :] = a[r] + b[c]
        top = _top_rows(cand_scr[...], K)
        z = jnp.ones_like(top[0])
        for r in range(1, K):
            z = z + jnp.exp(top[r] - top[0])
        thr_ref[h] = top[K - 1] - s[0]
        w0_ref[h] = 0.5 * jnp.exp(s[0] - a[0]) / z
        s1_ref[h] = s[1]
        e1_ref[h] = jnp.exp(s[1] - b[0])


def _route(u2, wq_t, keys, tb):
    t, d = u2.shape
    n_pairs = sum(1 for r in range(P_TOPK) for c in range(P_TOPK) if (r + 1) * (c + 1) <= P_TOPK)
    n_cand = -(-n_pairs // SUBLANES) * SUBLANES
    out = jax.ShapeDtypeStruct((P_HEADS, P_NKEYS, t), F32)
    ospec = pl.BlockSpec((P_HEADS, P_NKEYS, tb), lambda i: (0, 0, i))
    return pl.pallas_call(
        _route_kernel,
        out_shape=(out, out, out, out),
        grid=(t // tb,),
        in_specs=[pl.BlockSpec((tb, d), lambda i: (i, 0)),
                  pl.BlockSpec(wq_t.shape, lambda i: (0, 0)),
                  pl.BlockSpec(keys.shape, lambda i: (0, 0, 0))],
        out_specs=(ospec, ospec, ospec, ospec),
        scratch_shapes=[pltpu.VMEM((n_cand, tb), F32)],
        compiler_params=_params("parallel"),
        name="route",
    )(u2, wq_t, keys)


PEER_JB = 32
PEER_IB = 4


def _peer_step(u2_ref, u_ref, vt_ref, thr_ref, w0_ref, s1_ref, e1_ref, acc_scr,
               h_in, h_out, p_in, p_out):
    eb, d = u_ref.shape
    tb = u2_ref.shape[0]
    n_grp = eb // (PEER_IB * P_NKEYS)
    for ib in range(n_grp):
        er = slice(ib * (eb // n_grp), (ib + 1) * (eb // n_grp))
        dr = slice(ib * (d // n_grp), (ib + 1) * (d // n_grp))
        h_out[er, :] = _dot_nt(u_ref[er, :], u2_ref[...])
        acc_scr[dr, :] += _dot(vt_ref[dr, :], p_in[...])
        for tc in range(tb // LANES):
            lanes = slice(tc * LANES, (tc + 1) * LANES)
            for jb in range(P_NKEYS // PEER_JB):
                jrows = slice(jb * PEER_JB, (jb + 1) * PEER_JB)
                gate = [None] * PEER_IB
                for h in range(P_HEADS):
                    s1 = s1_ref[h, jrows, lanes]
                    e1 = e1_ref[h, jrows, lanes]
                    for k in range(PEER_IB):
                        i = ib * PEER_IB + k
                        term = (jnp.where(s1 >= thr_ref[h, i:i + 1, lanes], e1, 0.0)
                                * w0_ref[h, i:i + 1, lanes])
                        gate[k] = term if gate[k] is None else gate[k] + term
                for k in range(PEER_IB):
                    r0 = (ib * PEER_IB + k) * P_NKEYS + jb * PEER_JB
                    hv = h_in[r0:r0 + PEER_JB, lanes]
                    act = hv * (1.0 + lax.erf(hv * (2.0 ** -0.5)))
                    p_out[r0:r0 + PEER_JB, lanes] = (act * gate[k]).astype(BF16)


def _peer_kernel(n_e, u2_ref, x1_ref, fg_ref, u_ref, vt_ref, thr_ref, w0_ref, s1_ref, e1_ref,
                 out_ref, acc_scr, h_scr, p_scr):
    f = pl.program_id(0)

    @pl.when(f == 0)
    def _():
        acc_scr[...] = jnp.zeros(acc_scr.shape, F32)
        h_scr[1] = jnp.zeros(h_scr.shape[1:], F32)
        p_scr[1] = jnp.zeros(p_scr.shape[1:], BF16)

    step = functools.partial(_peer_step, u2_ref, u_ref, vt_ref, thr_ref, w0_ref, s1_ref, e1_ref,
                             acc_scr)

    @pl.when(f % 2 == 0)
    def _():
        step(h_scr.at[1], h_scr.at[0], p_scr.at[1], p_scr.at[0])

    @pl.when(f % 2 == 1)
    def _():
        step(h_scr.at[0], h_scr.at[1], p_scr.at[0], p_scr.at[1])

    @pl.when((f >= 2) & ((f - 2) % n_e == n_e - 1))
    def _():
        x2 = x1_ref[...] + acc_scr[...].T
        out_ref[...] = (x2 * lax.rsqrt(jnp.mean(x2 * x2, axis=-1, keepdims=True) + EPS)
                        * fg_ref[...])
        acc_scr[...] = jnp.zeros(acc_scr.shape, F32)


def _peer(u2, x1, fg, u_emb, v_t, thr, w0, s1, e1, tb, eb):
    t, d = u2.shape
    n_e = u_emb.shape[0] // eb
    n_i = eb // P_NKEYS
    n = (t // tb) * n_e
    blk = lambda f, lag: jnp.clip(f - lag, 0, n - 1)
    tok = lambda lag: (lambda f: (blk(f, lag) // n_e, 0))
    gate_blk = lambda f: (0, blk(f, 1) % n_e, blk(f, 1) // n_e)
    key_blk = lambda f: (0, 0, blk(f, 1) // n_e)
    return pl.pallas_call(
        functools.partial(_peer_kernel, n_e),
        out_shape=jax.ShapeDtypeStruct((t, d), F32),
        grid=(n + 2,),
        in_specs=[pl.BlockSpec((tb, d), tok(0)),
                  pl.BlockSpec((tb, d), tok(2)),
                  pl.BlockSpec((1, d), lambda f: (0, 0)),
                  pl.BlockSpec((eb, d), lambda f: (blk(f, 0) % n_e, 0)),
                  pl.BlockSpec((d, eb), lambda f: (0, blk(f, 2) % n_e)),
                  pl.BlockSpec((P_HEADS, n_i, tb), gate_blk),
                  pl.BlockSpec((P_HEADS, n_i, tb), gate_blk),
                  pl.BlockSpec((P_HEADS, P_NKEYS, tb), key_blk),
                  pl.BlockSpec((P_HEADS, P_NKEYS, tb), key_blk)],
        out_specs=pl.BlockSpec((tb, d), tok(2)),
        scratch_shapes=[pltpu.VMEM((d, tb), F32), pltpu.VMEM((2, eb, tb), F32),
                        pltpu.VMEM((2, eb, tb), BF16)],
        compiler_params=_params("arbitrary"),
        name="peer",
    )(u2, x1, fg, u_emb, v_t, thr, w0, s1, e1)


def _tiles(t):
    pick = lambda pref: next(c for c in (pref, 1024, 512, 256, 128) if c <= pref and t % c == 0)
    return dict(inproj_tm=pick(2048), inproj_tn=1024, merge_tm=pick(512),
                route_tb=pick(256), peer_tb=pick(512), peer_eb=1024)


def kernel(x, norm1_g, w_in, b_in, conv_w, conv_b, m_head_g, lb_table, g_head_g, w_br_m, w_br_g,
           w_out, norm2_g, w_pq, sub_keys, u_emb, v_emb, final_g):
    nb, seq, d = x.shape
    t = nb * seq
    depth = w_in.shape[0]
    assert d == 1024 and seq % M_CHUNK == 0 and t % LANES == 0
    tl = _tiles(t)
    n_gate = 2 * M_HEADS
    g0 = 2 * M_QK + 2 * M_V
    assert (w_in.shape[2] - n_gate) % tl["inproj_tn"] == 0

    lb_all = jnp.cumsum(jax.nn.softmax(lb_table.astype(F32), axis=0), axis=0)
    x2 = x.reshape(t, d)
    row = lambda a: a.reshape(1, -1).astype(F32)

    for l in range(depth):
        w_main = jnp.concatenate([w_in[l][:, :g0], w_in[l][:, g0 + n_gate:]], axis=1).astype(BF16)
        b_main = jnp.concatenate([b_in[l][:g0], b_in[l][g0 + n_gate:]]).reshape(1, -1)
        w_gate = jnp.pad(w_in[l][:, g0:g0 + n_gate], ((0, 0), (0, LANES - n_gate))).astype(BF16)
        b_gate = jnp.pad(b_in[l][g0:g0 + n_gate], (0, LANES - n_gate)).reshape(1, -1)

        proj, gates = _inproj(x2, row(norm1_g[l]), w_main, b_main, w_gate, b_gate,
                              tl["inproj_tm"], tl["inproj_tn"])
        y_m = _mlstm(proj, gates, conv_w[l], row(conv_b[l]), row(m_head_g[l]),
                     nb, seq // M_CHUNK)
        y_g = _hgrn(proj, row(lb_all[l]), row(g_head_g[l]), nb, seq // G_CHUNK, g0 // G_W)
        x1, u2 = _merge(x2, y_m, y_g, proj, w_br_m[l].astype(BF16), w_br_g[l].astype(BF16),
                        w_out[l].astype(BF16), row(norm2_g[l]), tl["merge_tm"],
                        (g0 + 4 * G_W) // d)

        keys = sub_keys[l].reshape(2 * P_HEADS, P_NKEYS, -1).astype(BF16)
        thr, w0, s1, e1 = _route(u2, w_pq[l].T.astype(BF16), keys, tl["route_tb"])
        assert depth == 1
        x2 = _peer(u2, x1, row(final_g), u_emb[l].astype(BF16), v_emb[l].T.astype(BF16),
                   thr, w0, s1, e1, tl["peer_tb"], tl["peer_eb"])
    return x2.reshape(nb, seq, d)
```

```python
import functools

import jax
import jax.numpy as jnp
from jax import lax
from jax.experimental import pallas as pl
from jax.experimental.pallas import tpu as pltpu

F32 = jnp.float32
BF16 = jnp.bfloat16
HIGHEST = lax.Precision.HIGHEST

EPS = 1e-6
LOG2E = 1.4426950408889634
LANES = 128
SUBLANES = 8
VMEM_LIMIT = 48 * 1024 * 1024

M_HEADS, M_QK_DIM, M_V_DIM, M_CHUNK, CONV_W = 4, 256, 512, 128, 4
G_HEADS, G_DIM, G_CHUNK, G_SUB = 8, 128, 64, 16
P_HEADS, P_NKEYS, P_TOPK = 8, 128, 16
M_QK = M_HEADS * M_QK_DIM
M_V = M_HEADS * M_V_DIM
G_W = G_HEADS * G_DIM


def _params(*sem):
    return pltpu.CompilerParams(dimension_semantics=sem, vmem_limit_bytes=VMEM_LIMIT)


def _dot(a, b):
    return jnp.dot(a, b, preferred_element_type=F32)


def _dot_nt(a, b):
    return lax.dot_general(a, b, (((1,), (1,)), ((), ())), preferred_element_type=F32)


def _dot_tn(a, b):
    return lax.dot_general(a, b, (((0,), (0,)), ((), ())), preferred_element_type=F32)


def _sigmoid(x):
    return 1.0 / (1.0 + jnp.exp2(x * (-LOG2E)))


def _tril(n):
    r = lax.broadcasted_iota(jnp.int32, (n, n), 0)
    c = lax.broadcasted_iota(jnp.int32, (n, n), 1)
    return (r >= c)


def _inproj_kernel(x_ref, g_ref, w_ref, b_ref, wg_ref, bg_ref, out_ref, gate_ref, u_scr):
    @pl.when(pl.program_id(1) == 0)
    def _():
        x = x_ref[...]
        u = x * lax.rsqrt(jnp.mean(x * x, axis=-1, keepdims=True) + EPS) * g_ref[...]
        u_scr[...] = u.astype(BF16)
        gate_ref[...] = _dot(u.astype(BF16), wg_ref[...]) + bg_ref[...]

    out_ref[...] = (_dot(u_scr[...], w_ref[...]) + b_ref[...]).astype(BF16)


def _inproj(x2, g, w, b, wg, bg, tm, tn):
    t, d = x2.shape
    n = w.shape[1]
    return pl.pallas_call(
        _inproj_kernel,
        out_shape=(jax.ShapeDtypeStruct((t, n), BF16), jax.ShapeDtypeStruct((t, LANES), F32)),
        grid=(t // tm, n // tn),
        in_specs=[
            pl.BlockSpec((tm, d), lambda i, j: (i, 0)),
            pl.BlockSpec((1, d), lambda i, j: (0, 0)),
            pl.BlockSpec((d, tn), lambda i, j: (0, j)),
            pl.BlockSpec((1, tn), lambda i, j: (0, j)),
            pl.BlockSpec((d, LANES), lambda i, j: (0, 0)),
            pl.BlockSpec((1, LANES), lambda i, j: (0, 0)),
        ],
        out_specs=(pl.BlockSpec((tm, tn), lambda i, j: (i, j)),
                   pl.BlockSpec((tm, LANES), lambda i, j: (i, 0))),
        scratch_shapes=[pltpu.VMEM((tm, d), BF16)],
        compiler_params=_params("parallel", "arbitrary"),
        name="inproj",
    )(x2, g, w, b, wg, bg)


def _mlstm_kernel(qk_ref, v_ref, o_ref, gt_ref, cw_ref, cb_ref, hg_ref, y_ref,
                  ext_scr, c_scr, m_scr):
    L = M_CHUNK

    @pl.when(pl.program_id(1) == 0)
    def _():
        ext_scr[0:SUBLANES, :] = jnp.zeros((SUBLANES, 2 * M_QK), F32)
        c_scr[...] = jnp.zeros(c_scr.shape, F32)
        m_scr[...] = jnp.zeros(m_scr.shape, F32)

    cur = qk_ref[...].astype(F32)
    ext_scr[SUBLANES:SUBLANES + L, :] = cur
    acc = cb_ref[...] + cw_ref[CONV_W - 1:CONV_W, :] * cur
    for j in range(CONV_W - 1):
        off = SUBLANES - (CONV_W - 1) + j
        acc = acc + cw_ref[j:j + 1, :] * ext_scr[off:off + L, :]
    ext_scr[0:SUBLANES, :] = cur[L - SUBLANES:L, :]
    qk = acc * _sigmoid(acc)

    gt = gt_ref[...]
    lf = jnp.minimum(gt, 0.0) - jnp.log(1.0 + jnp.exp(-jnp.abs(gt)))
    tri = _tril(L)
    bcols = jnp.dot(tri.astype(F32), lf, precision=HIGHEST, preferred_element_type=F32)
    gt_t = gt.T
    b_t = bcols.T
    ones_ext = jnp.ones((L, LANES), BF16)

    for h in range(M_HEADS):
        icol = gt[:, h:h + 1]
        bcol = bcols[:, M_HEADS + h:M_HEADS + h + 1]
        irow = gt_t[h:h + 1, :]
        brow = b_t[M_HEADS + h:M_HEADS + h + 1, :]
        m_prev = m_scr[h, 0:1, 0:1]
        b_last = bcol[L - 1:L, :]

        dmat = jnp.where(tri, bcol - brow + irow, -jnp.inf)
        inter = bcol + m_prev
        m_t = jnp.maximum(inter, jnp.max(dmat, axis=-1, keepdims=True))
        dexp = jnp.exp(dmat - m_t)
        inter_w = jnp.exp(inter - m_t)

        q = (qk[:, h * M_QK_DIM:(h + 1) * M_QK_DIM] * (M_QK_DIM ** -0.5)).astype(BF16)
        kf = qk[:, M_QK + h * M_QK_DIM:M_QK + (h + 1) * M_QK_DIM]
        v_ext = jnp.concatenate([v_ref[:, h * M_V_DIM:(h + 1) * M_V_DIM], ones_ext], axis=1)

        scores = (_dot_nt(q, kf.astype(BF16)) * dexp).astype(BF16)
        c_old = c_scr[h]
        nd = _dot(scores, v_ext) + inter_w * _dot(q, c_old.astype(BF16))
        den = nd[:, M_V_DIM:]
        inv = 1.0 / jnp.maximum(jnp.abs(den), jnp.exp(-m_t))
        hs = [nd[:, j * LANES:(j + 1) * LANES] * inv for j in range(M_V_DIM // LANES)]
        hh = jnp.concatenate(hs, axis=1)

        m_new = m_t[L - 1:L, :]
        w_s = jnp.exp(b_last - bcol + icol - m_new)
        decay = jnp.exp(b_last + m_prev - m_new)
        kw = (kf * w_s).astype(BF16)
        c_scr[h] = decay * c_old + _dot_tn(kw, v_ext)
        m_scr[h] = jnp.broadcast_to(m_new, (SUBLANES, LANES))

        hn = hh * lax.rsqrt(jnp.mean(hh * hh, axis=-1, keepdims=True) + EPS)
        sl = slice(h * M_V_DIM, (h + 1) * M_V_DIM)
        og = _sigmoid(o_ref[:, sl].astype(F32))
        y_ref[:, sl] = (og * hn * hg_ref[:, sl]).astype(BF16)


def _mlstm(proj, gates, conv_w, conv_b, head_g, nb, nc):
    t = proj.shape[0]
    L = M_CHUNK
    row = lambda b, c: b * nc + c
    return pl.pallas_call(
        _mlstm_kernel,
        out_shape=jax.ShapeDtypeStruct((t, M_V), BF16),
        grid=(nb, nc),
        in_specs=[
            pl.BlockSpec((L, 2 * M_QK), lambda b, c: (row(b, c), 0)),
            pl.BlockSpec((L, M_V), lambda b, c: (row(b, c), 1)),
            pl.BlockSpec((L, M_V), lambda b, c: (row(b, c), 2)),
            pl.BlockSpec((L, LANES), lambda b, c: (row(b, c), 0)),
            pl.BlockSpec((CONV_W, 2 * M_QK), lambda b, c: (0, 0)),
            pl.BlockSpec((1, 2 * M_QK), lambda b, c: (0, 0)),
            pl.BlockSpec((1, M_V), lambda b, c: (0, 0)),
        ],
        out_specs=pl.BlockSpec((L, M_V), lambda b, c: (row(b, c), 0)),
        scratch_shapes=[
            pltpu.VMEM((SUBLANES + L, 2 * M_QK), F32),
            pltpu.VMEM((M_HEADS, M_QK_DIM, M_V_DIM + LANES), F32),
            pltpu.VMEM((M_HEADS, SUBLANES, LANES), F32),
        ],
        compiler_params=_params("parallel", "arbitrary"),
        name="mlstm",
    )(proj, proj, proj, gates, conv_w, conv_b, head_g)


def _hgrn_kernel(gf_ref, gi_ref, gq_ref, gg_ref, lb_ref, hg_ref, y_ref, st_scr):
    L, SUB, NSUB = G_CHUNK, G_SUB, G_CHUNK // G_SUB

    @pl.when(pl.program_id(1) == 0)
    def _():
        st_scr[...] = jnp.zeros(st_scr.shape, F32)

    lb = lb_ref[...]
    f = lb + (1.0 - lb) * _sigmoid(gf_ref[...].astype(F32))
    k_all = 1.0 - f
    cum_all = jnp.dot(_tril(L).astype(F32), jnp.log(f), precision=HIGHEST,
                      preferred_element_type=F32)
    gq = gq_ref[...].astype(F32)
    q_all = gq * _sigmoid(gq)

    row = lax.broadcasted_iota(jnp.int32, (L, G_DIM), 0)
    row_sub = lax.broadcasted_iota(jnp.int32, (SUB, G_DIM), 0)
    ones_sq = jnp.ones((G_DIM, G_DIM), BF16)

    for h in range(G_HEADS):
        sl = slice(h * G_DIM, (h + 1) * G_DIM)
        cum, q, k = cum_all[:, sl], q_all[:, sl], k_all[:, sl]
        iv = gi_ref[:, sl]
        ivf = iv.astype(F32)
        ends = [cum[SUB * b + SUB - 1:SUB * b + SUB, :] for b in range(NSUB)]
        last = ends[NSUB - 1]
        end_of_row = jnp.concatenate([jnp.broadcast_to(e, (SUB, G_DIM)) for e in ends], axis=0)

        khat = k * jnp.exp(end_of_row - cum)
        qs, ks = [], []
        for b in range(NSUB - 1):
            lo = SUB * (b + 1)
            qs.append(jnp.where(row >= lo, q * jnp.exp(jnp.minimum(cum - ends[b], 0.0)), 0.0))
            ks.append(jnp.where((row >= lo - SUB) & (row < lo), khat, 0.0))
        att = _dot_nt(jnp.concatenate(qs, axis=1).astype(BF16),
                      jnp.concatenate(ks, axis=1).astype(BF16))
        o = _dot(att.astype(BF16), iv)

        st = st_scr[h]
        o = o + _dot_nt((q * jnp.exp(cum)).astype(BF16), st.astype(BF16))

        o_diag = []
        H = SUBLANES
        for a in range(NSUB):
            r0 = SUB * a
            q_a, k_a, c_a, i_a = (z[r0:r0 + SUB, :] for z in (q, k, cum, ivf))
            ms = []
            for s in range(SUB):
                lo = 0 if s < H else H
                diff = c_a[lo:, :] - c_a[s:s + 1, :]
                part = jnp.where(row_sub[lo:lo + H, :] >= s, diff[0:H, :], -jnp.inf)
                diff = jnp.concatenate([part, diff[H:, :]], axis=0) if lo == 0 else part
                ms.append(q_a[lo:, :] * (k_a[s:s + 1, :] * jnp.exp(diff)))
            red = _dot(jnp.concatenate(ms, axis=0).astype(BF16), ones_sq)
            acc_lo = red[0:H, :] * i_a[0:1, :]
            acc_hi = red[H:SUB, :] * i_a[0:1, :]
            for s in range(1, H):
                acc_lo = acc_lo + red[s * SUB:s * SUB + H, :] * i_a[s:s + 1, :]
                acc_hi = acc_hi + red[s * SUB + H:(s + 1) * SUB, :] * i_a[s:s + 1, :]
            for s in range(H, SUB):
                off = H * SUB + (s - H) * H
                acc_hi = acc_hi + red[off:off + H, :] * i_a[s:s + 1, :]
            o_diag += [acc_lo, acc_hi]
        o = o + jnp.concatenate(o_diag, axis=0)

        kd = (k * jnp.exp(last - cum)).astype(BF16)
        st_scr[h] = jnp.exp(last) * st + _dot_tn(iv, kd)

        hn = o * lax.rsqrt(jnp.mean(o * o, axis=-1, keepdims=True) + EPS)
        gg = gg_ref[:, sl].astype(F32)
        y_ref[:, sl] = (hn * hg_ref[:, sl] * (gg * _sigmoid(gg))).astype(BF16)


def _hgrn(proj, lb, head_g, nb, nc, col0):
    t = proj.shape[0]
    L = G_CHUNK
    row = lambda b, c: b * nc + c
    spec = lambda j: pl.BlockSpec((L, G_W), lambda b, c: (row(b, c), col0 + j))
    return pl.pallas_call(
        _hgrn_kernel,
        out_shape=jax.ShapeDtypeStruct((t, G_W), BF16),
        grid=(nb, nc),
        in_specs=[spec(0), spec(1), spec(2), spec(3),
                  pl.BlockSpec((1, G_W), lambda b, c: (0, 0)),
                  pl.BlockSpec((1, G_W), lambda b, c: (0, 0))],
        out_specs=pl.BlockSpec((L, G_W), lambda b, c: (row(b, c), 0)),
        scratch_shapes=[pltpu.VMEM((G_HEADS, G_DIM, G_DIM), F32)],
        compiler_params=_params("parallel", "arbitrary"),
        name="hgrn2",
    )(proj, proj, proj, proj, lb, head_g)


def _merge_kernel(x_ref, ym_ref, yg_ref, am_ref, ag_ref, wm_ref, wg_ref, wo_ref, g2_ref,
                  x1_ref, u2_ref):
    zm = _dot(ym_ref[...], wm_ref[...])
    zg = _dot(yg_ref[...], wg_ref[...])
    z = _sigmoid(am_ref[...].astype(F32)) * zm + _sigmoid(ag_ref[...].astype(F32)) * zg
    x1 = x_ref[...] + _dot(z.astype(BF16), wo_ref[...])
    x1_ref[...] = x1
    u2 = x1 * lax.rsqrt(jnp.mean(x1 * x1, axis=-1, keepdims=True) + EPS) * g2_ref[...]
    u2_ref[...] = u2.astype(BF16)


def _merge(x2, ym, yg, proj, wm, wg, wo, g2, tm, col_am):
    t, d = x2.shape
    rows = lambda w: pl.BlockSpec((tm, w), lambda i: (i, 0))
    full = lambda a: pl.BlockSpec(a.shape, lambda i: (0, 0))
    return pl.pallas_call(
        _merge_kernel,
        out_shape=(jax.ShapeDtypeStruct((t, d), F32), jax.ShapeDtypeStruct((t, d), BF16)),
        grid=(t // tm,),
        in_specs=[rows(d), rows(M_V), rows(G_W),
                  pl.BlockSpec((tm, d), lambda i: (i, col_am)),
                  pl.BlockSpec((tm, d), lambda i: (i, col_am + 1)),
                  full(wm), full(wg), full(wo), full(g2)],
        out_specs=(rows(d), rows(d)),
        compiler_params=_params("parallel"),
        name="merge",
    )(x2, ym, yg, proj, proj, wm, wg, wo, g2)


def _sort16_pairs():
    def merge(lo, hi, r):
        step = r * 2
        if step < hi - lo:
            yield from merge(lo, hi, step)
            yield from merge(lo + r, hi, step)
            yield from [(i, i + r) for i in range(lo + r, hi - r, step)]
        else:
            yield (lo, lo + r)

    def sort(lo, hi):
        if hi - lo >= 1:
            mid = lo + (hi - lo) // 2
            yield from sort(lo, mid)
            yield from sort(mid + 1, hi)
            yield from merge(lo, hi, 1)

    return list(sort(0, P_TOPK - 1))


def _top_rows(vals, k):
    out = []
    for _ in range(k):
        m = jnp.max(vals, axis=0, keepdims=True)
        out.append(m)
        vals = jnp.where(vals == m, -jnp.inf, vals)
    return out


def _top16_of_128(vals):
    K = P_TOPK
    v = [vals[SUBLANES * g:SUBLANES * (g + 1), :] for g in range(K)]
    for a, b in _sort16_pairs():
        v[a], v[b] = jnp.maximum(v[a], v[b]), jnp.minimum(v[a], v[b])
    out = []
    for r in range(K):
        m = jnp.max(v[0], axis=0, keepdims=True)
        out.append(m)
        if r == K - 1:
            break
        hit = v[0] == m
        depth = K - 1 - r
        for i in range(depth):
            v[i] = jnp.where(hit, v[i + 1], v[i])
    return out


def _route_kernel(u2_ref, wq_ref, keys_ref, thr_ref, w0_ref, s1_ref, e1_ref, cand_scr):
    K = P_TOPK
    tb = u2_ref.shape[0]
    q_t = _dot_nt(wq_ref[...], u2_ref[...])
    pairs = [(r, c) for r in range(K) for c in range(K) if (r + 1) * (c + 1) <= K]
    n_pad = cand_scr.shape[0] - len(pairs)
    cand_scr[len(pairs):, :] = jnp.full((n_pad, tb), -jnp.inf, F32)

    for h in range(P_HEADS):
        s = []
        for p in range(2):
            r0 = (2 * h + p) * P_NKEYS
            s.append(_dot(keys_ref[2 * h + p], q_t[r0:r0 + P_NKEYS, :].astype(BF16)))
        a = _top16_of_128(s[0])
        b = _top16_of_128(s[1])
        for n, (r, c) in enumerate(pairs):
            cand_scr[n:n + 1, :] = a[r] + b[c]
        top = _top_rows(cand_scr[...], K)
        z = jnp.ones_like(top[0])
        for r in range(1, K):
            z = z + jnp.exp(top[r] - top[0])
        thr_ref[h] = top[K - 1] - s[0]
        w0_ref[h] = 0.5 * jnp.exp(s[0] - a[0]) / z
        s1_ref[h] = s[1]
        e1_ref[h] = jnp.exp(s[1] - b[0])


def _route(u2, wq_t, keys, tb):
    t, d = u2.shape
    n_pairs = sum(1 for r in range(P_TOPK) for c in range(P_TOPK) if (r + 1) * (c + 1) <= P_TOPK)
    n_cand = -(-n_pairs // SUBLANES) * SUBLANES
    out = jax.ShapeDtypeStruct((P_HEADS, P_NKEYS, t), F32)
    ospec = pl.BlockSpec((P_HEADS, P_NKEYS, tb), lambda i: (0, 0, i))
    return pl.pallas_call(
        _route_kernel,
        out_shape=(out, out, out, out),
        grid=(t // tb,),
        in_specs=[pl.BlockSpec((tb, d), lambda i: (i, 0)),
                  pl.BlockSpec(wq_t.shape, lambda i: (0, 0)),
                  pl.BlockSpec(keys.shape, lambda i: (0, 0, 0))],
        out_specs=(ospec, ospec, ospec, ospec),
        scratch_shapes=[pltpu.VMEM((n_cand, tb), F32)],
        compiler_params=_params("parallel"),
        name="route",
    )(u2, wq_t, keys)


PEER_JB = 32
PEER_IB = 4
MXU_TILE = 256
N_MXU = 2
MXU_ROWS = 64
ACC_ROWS_PER_ENTRY = 4


def _gate_tile(thr_ref, w0_ref, s1_ref, e1_ref, h_in, p_out, ib, tc, jb):
    lanes = slice(tc * LANES, (tc + 1) * LANES)
    jrows = slice(jb * PEER_JB, (jb + 1) * PEER_JB)
    gate = [None] * PEER_IB
    for h in range(P_HEADS):
        s1 = s1_ref[h, jrows, lanes]
        e1 = e1_ref[h, jrows, lanes]
        for k in range(PEER_IB):
            i = ib * PEER_IB + k
            term = (jnp.where(s1 >= thr_ref[h, i:i + 1, lanes], e1, 0.0)
                    * w0_ref[h, i:i + 1, lanes])
            gate[k] = term if gate[k] is None else gate[k] + term
    for k in range(PEER_IB):
        r0 = (ib * PEER_IB + k) * P_NKEYS + jb * PEER_JB
        hv = h_in[r0:r0 + PEER_JB, lanes]
        act = hv * (1.0 + lax.erf(hv * (2.0 ** -0.5)))
        p_out[r0:r0 + PEER_JB, lanes] = (act * gate[k]).astype(BF16)


def _peer_step(u2_ref, u_ref, vt_ref, thr_ref, w0_ref, s1_ref, e1_ref, acc_scr,
               h_in, h_out, p_in, p_out):
    f = pl.program_id(0)
    eb, d = u_ref.shape
    tb = u2_ref.shape[0]
    assert tb == N_MXU * MXU_TILE and eb == d
    n_kt = d // MXU_TILE
    n_c = eb // MXU_ROWS
    per_phase = n_kt * n_c
    tiles = [(ib, tc, jb) for ib in range(eb // (PEER_IB * P_NKEYS))
             for tc in range(tb // LANES) for jb in range(P_NKEYS // PEER_JB)]
    n_reg = len(tiles)
    assert (2 * per_phase) % n_reg == 0
    per_reg = 2 * per_phase // n_reg

    def weights(w):
        phase, kt = divmod(w, n_kt)
        ks = slice(kt * MXU_TILE, (kt + 1) * MXU_TILE)
        for q in range(N_MXU):
            ts = slice(q * MXU_TILE, (q + 1) * MXU_TILE)
            if phase == 0:
                pltpu.matmul_push_rhs(u2_ref[ts, ks], staging_register=w % 2, mxu_index=q,
                                      transpose=True)
            else:
                pltpu.matmul_push_rhs(p_in[ks, ts], staging_register=w % 2, mxu_index=q)

    def piece(idx):
        w, c = divmod(idx, n_c)
        phase, kt = divmod(w, n_kt)
        rows = slice(c * MXU_ROWS, (c + 1) * MXU_ROWS)
        ks = slice(kt * MXU_TILE, (kt + 1) * MXU_TILE)
        addr = c * (MXU_ROWS // ACC_ROWS_PER_ENTRY)
        if c == n_c // 2 and w + 1 < 2 * n_kt:
            weights(w + 1)
        for q in range(N_MXU):
            ts = slice(q * MXU_TILE, (q + 1) * MXU_TILE)
            if kt == 0 and phase == 0:
                acc_scr[rows, ts] += pltpu.matmul_pop(addr, (MXU_ROWS, MXU_TILE), F32, mxu_index=q)
            if kt == 0 and phase == 1:
                h_out[rows, ts] = pltpu.matmul_pop(addr, (MXU_ROWS, MXU_TILE), F32, mxu_index=q)
            lhs = u_ref[rows, ks] if phase == 0 else vt_ref[rows, ks]
            pltpu.matmul_acc_lhs(addr, lhs, mxu_index=q,
                                 load_staged_rhs=(w % 2) if c == 0 else None)

    weights(0)
    for r in range(n_reg):
        @pl.when(f >= -r)
        def _(r=r):
            for idx in range(r * per_reg, (r + 1) * per_reg):
                piece(idx)
            _gate_tile(thr_ref, w0_ref, s1_ref, e1_ref, h_in, p_out, *tiles[r])


def _peer_kernel(n_e, n_steps, u2_ref, x1_ref, fg_ref, u_ref, vt_ref, thr_ref, w0_ref, s1_ref,
                 e1_ref, out_ref, acc_scr, h_scr, p_scr):
    f = pl.program_id(0)
    full = (h_scr.shape[1], MXU_TILE)

    @pl.when(f == 0)
    def _():
        acc_scr[...] = jnp.zeros(acc_scr.shape, F32)
        p_scr[1] = jnp.zeros(p_scr.shape[1:], BF16)
        for q in range(N_MXU):
            h_scr[0, :, q * MXU_TILE:(q + 1) * MXU_TILE] = pltpu.matmul_pop(0, full, F32, mxu_index=q)
        h_scr[1] = jnp.zeros(h_scr.shape[1:], F32)

    step = functools.partial(_peer_step, u2_ref, u_ref, vt_ref, thr_ref, w0_ref, s1_ref, e1_ref,
                             acc_scr)

    @pl.when(f % 2 == 0)
    def _():
        step(h_scr.at[1], h_scr.at[0], p_scr.at[1], p_scr.at[0])

    @pl.when(f % 2 == 1)
    def _():
        step(h_scr.at[0], h_scr.at[1], p_scr.at[0], p_scr.at[1])

    @pl.when((f >= 3) & ((f - 3) % n_e == n_e - 1))
    def _():
        x2 = x1_ref[...] + acc_scr[...].T
        out_ref[...] = (x2 * lax.rsqrt(jnp.mean(x2 * x2, axis=-1, keepdims=True) + EPS)
                        * fg_ref[...])
        acc_scr[...] = jnp.zeros(acc_scr.shape, F32)

    @pl.when(f == n_steps - 1)
    def _():
        for q in range(N_MXU):
            h_scr[0, :, q * MXU_TILE:(q + 1) * MXU_TILE] = pltpu.matmul_pop(0, full, F32, mxu_index=q)


def _peer(u2, x1, fg, u_emb, v_t, thr, w0, s1, e1, tb, eb):
    t, d = u2.shape
    n_e = u_emb.shape[0] // eb
    n_i = eb // P_NKEYS
    n = (t // tb) * n_e
    n_steps = n + 3
    blk = lambda f, lag: jnp.clip(f - lag, 0, n - 1)
    tok = lambda lag: (lambda f: (blk(f, lag) // n_e, 0))
    gate_blk = lambda f: (0, blk(f, 1) % n_e, blk(f, 1) // n_e)
    key_blk = lambda f: (0, 0, blk(f, 1) // n_e)
    return pl.pallas_call(
        functools.partial(_peer_kernel, n_e, n_steps),
        out_shape=jax.ShapeDtypeStruct((t, d), F32),
        grid=(n_steps,),
        in_specs=[pl.BlockSpec((tb, d), tok(0)),
                  pl.BlockSpec((tb, d), tok(3)),
                  pl.BlockSpec((1, d), lambda f: (0, 0)),
                  pl.BlockSpec((eb, d), lambda f: (blk(f, 0) % n_e, 0)),
                  pl.BlockSpec((d, eb), lambda f: (0, blk(f, 2) % n_e)),
                  pl.BlockSpec((P_HEADS, n_i, tb), gate_blk),
                  pl.BlockSpec((P_HEADS, n_i, tb), gate_blk),
                  pl.BlockSpec((P_HEADS, P_NKEYS, tb), key_blk),
                  pl.BlockSpec((P_HEADS, P_NKEYS, tb), key_blk)],
        out_specs=pl.BlockSpec((tb, d), tok(3)),
        scratch_shapes=[pltpu.VMEM((d, tb), F32), pltpu.VMEM((2, eb, tb), F32),
                        pltpu.VMEM((2, eb, tb), BF16)],
        compiler_params=_params("arbitrary"),
        name="peer",
    )(u2, x1, fg, u_emb, v_t, thr, w0, s1, e1)


def _tiles(t):
    pick = lambda pref: next(c for c in (pref, 1024, 512, 256, 128) if c <= pref and t % c == 0)
    return dict(inproj_tm=pick(2048), inproj_tn=1024, merge_tm=pick(512),
                route_tb=pick(256), peer_tb=pick(512), peer_eb=1024)


def kernel(x, norm1_g, w_in, b_in, conv_w, conv_b, m_head_g, lb_table, g_head_g, w_br_m, w_br_g,
           w_out, norm2_g, w_pq, sub_keys, u_emb, v_emb, final_g):
    nb, seq, d = x.shape
    t = nb * seq
    depth = w_in.shape[0]
    assert d == 1024 and seq % M_CHUNK == 0 and t % LANES == 0
    tl = _tiles(t)
    n_gate = 2 * M_HEADS
    g0 = 2 * M_QK + 2 * M_V
    assert (w_in.shape[2] - n_gate) % tl["inproj_tn"] == 0

    lb_all = jnp.cumsum(jax.nn.softmax(lb_table.astype(F32), axis=0), axis=0)
    x2 = x.reshape(t, d)
    row = lambda a: a.reshape(1, -1).astype(F32)

    for l in range(depth):
        w_main = jnp.concatenate([w_in[l][:, :g0], w_in[l][:, g0 + n_gate:]], axis=1).astype(BF16)
        b_main = jnp.concatenate([b_in[l][:g0], b_in[l][g0 + n_gate:]]).reshape(1, -1)
        w_gate = jnp.pad(w_in[l][:, g0:g0 + n_gate], ((0, 0), (0, LANES - n_gate))).astype(BF16)
        b_gate = jnp.pad(b_in[l][g0:g0 + n_gate], (0, LANES - n_gate)).reshape(1, -1)

        proj, gates = _inproj(x2, row(norm1_g[l]), w_main, b_main, w_gate, b_gate,
                              tl["inproj_tm"], tl["inproj_tn"])
        y_m = _mlstm(proj, gates, conv_w[l], row(conv_b[l]), row(m_head_g[l]),
                     nb, seq // M_CHUNK)
        y_g = _hgrn(proj, row(lb_all[l]), row(g_head_g[l]), nb, seq // G_CHUNK, g0 // G_W)
        x1, u2 = _merge(x2, y_m, y_g, proj, w_br_m[l].astype(BF16), w_br_g[l].astype(BF16),
                        w_out[l].astype(BF16), row(norm2_g[l]), tl["merge_tm"],
                        (g0 + 4 * G_W) // d)

        keys = sub_keys[l].reshape(2 * P_HEADS, P_NKEYS, -1).astype(BF16)
        thr, w0, s1, e1 = _route(u2, w_pq[l].T.astype(BF16), keys, tl["route_tb"])
        assert depth == 1
        x2 = _peer(u2, x1, row(final_g), u_emb[l].astype(BF16), v_emb[l].T.astype(BF16),
                   thr, w0, s1, e1, tl["peer_tb"], tl["peer_eb"])
    return x2.reshape(nb, seq, d)
```

```python
import functools

import jax
import jax.numpy as jnp
from jax import lax
from jax.experimental import pallas as pl
from jax.experimental.pallas import tpu as pltpu

F32 = jnp.float32
BF16 = jnp.bfloat16
HIGHEST = lax.Precision.HIGHEST

EPS = 1e-6
LOG2E = 1.4426950408889634
LANES = 128
SUBLANES = 8
VMEM_LIMIT = 48 * 1024 * 1024

M_HEADS, M_QK_DIM, M_V_DIM, M_CHUNK, CONV_W = 4, 256, 512, 128, 4
G_HEADS, G_DIM, G_CHUNK, G_SUB = 8, 128, 64, 16
P_HEADS, P_NKEYS, P_TOPK = 8, 128, 16
M_QK = M_HEADS * M_QK_DIM
M_V = M_HEADS * M_V_DIM
G_W = G_HEADS * G_DIM


def _params(*sem):
    return pltpu.CompilerParams(dimension_semantics=sem, vmem_limit_bytes=VMEM_LIMIT)


def _dot(a, b):
    return jnp.dot(a, b, preferred_element_type=F32)


def _dot_nt(a, b):
    return lax.dot_general(a, b, (((1,), (1,)), ((), ())), preferred_element_type=F32)


def _dot_tn(a, b):
    return lax.dot_general(a, b, (((0,), (0,)), ((), ())), preferred_element_type=F32)


def _sigmoid(x):
    return 1.0 / (1.0 + jnp.exp2(x * (-LOG2E)))


def _tril(n):
    r = lax.broadcasted_iota(jnp.int32, (n, n), 0)
    c = lax.broadcasted_iota(jnp.int32, (n, n), 1)
    return (r >= c)


def _inproj_kernel(x_ref, g_ref, w_ref, b_ref, wg_ref, bg_ref, out_ref, gate_ref, u_scr):
    @pl.when(pl.program_id(1) == 0)
    def _():
        x = x_ref[...]
        u = x * lax.rsqrt(jnp.mean(x * x, axis=-1, keepdims=True) + EPS) * g_ref[...]
        u_scr[...] = u.astype(BF16)
        gate_ref[...] = _dot(u.astype(BF16), wg_ref[...]) + bg_ref[...]

    out_ref[...] = (_dot(u_scr[...], w_ref[...]) + b_ref[...]).astype(BF16)


def _inproj(x2, g, w, b, wg, bg, tm, tn):
    t, d = x2.shape
    n = w.shape[1]
    return pl.pallas_call(
        _inproj_kernel,
        out_shape=(jax.ShapeDtypeStruct((t, n), BF16), jax.ShapeDtypeStruct((t, LANES), F32)),
        grid=(t // tm, n // tn),
        in_specs=[
            pl.BlockSpec((tm, d), lambda i, j: (i, 0)),
            pl.BlockSpec((1, d), lambda i, j: (0, 0)),
            pl.BlockSpec((d, tn), lambda i, j: (0, j)),
            pl.BlockSpec((1, tn), lambda i, j: (0, j)),
            pl.BlockSpec((d, LANES), lambda i, j: (0, 0)),
            pl.BlockSpec((1, LANES), lambda i, j: (0, 0)),
        ],
        out_specs=(pl.BlockSpec((tm, tn), lambda i, j: (i, j)),
                   pl.BlockSpec((tm, LANES), lambda i, j: (i, 0))),
        scratch_shapes=[pltpu.VMEM((tm, d), BF16)],
        compiler_params=_params("parallel", "arbitrary"),
        name="inproj",
    )(x2, g, w, b, wg, bg)


def _mlstm_kernel(qk_ref, v_ref, o_ref, gt_ref, cw_ref, cb_ref, hg_ref, y_ref,
                  ext_scr, c_scr, m_scr):
    L = M_CHUNK

    @pl.when(pl.program_id(1) == 0)
    def _():
        ext_scr[0:SUBLANES, :] = jnp.zeros((SUBLANES, 2 * M_QK), F32)
        c_scr[...] = jnp.zeros(c_scr.shape, F32)
        m_scr[...] = jnp.zeros(m_scr.shape, F32)

    cur = qk_ref[...].astype(F32)
    ext_scr[SUBLANES:SUBLANES + L, :] = cur
    acc = cb_ref[...] + cw_ref[CONV_W - 1:CONV_W, :] * cur
    for j in range(CONV_W - 1):
        off = SUBLANES - (CONV_W - 1) + j
        acc = acc + cw_ref[j:j + 1, :] * ext_scr[off:off + L, :]
    ext_scr[0:SUBLANES, :] = cur[L - SUBLANES:L, :]
    qk = acc * _sigmoid(acc)

    gt = gt_ref[...]
    lf = jnp.minimum(gt, 0.0) - jnp.log(1.0 + jnp.exp(-jnp.abs(gt)))
    tri = _tril(L)
    bcols = jnp.dot(tri.astype(F32), lf, precision=HIGHEST, preferred_element_type=F32)
    gt_t = gt.T
    b_t = bcols.T
    ones_ext = jnp.ones((L, LANES), BF16)

    for h in range(M_HEADS):
        icol = gt[:, h:h + 1]
        bcol = bcols[:, M_HEADS + h:M_HEADS + h + 1]
        irow = gt_t[h:h + 1, :]
        brow = b_t[M_HEADS + h:M_HEADS + h + 1, :]
        m_prev = m_scr[h, 0:1, 0:1]
        b_last = bcol[L - 1:L, :]

        dmat = jnp.where(tri, bcol - brow + irow, -jnp.inf)
        inter = bcol + m_prev
        m_t = jnp.maximum(inter, jnp.max(dmat, axis=-1, keepdims=True))
        dexp = jnp.exp(dmat - m_t)
        inter_w = jnp.exp(inter - m_t)

        q = (qk[:, h * M_QK_DIM:(h + 1) * M_QK_DIM] * (M_QK_DIM ** -0.5)).astype(BF16)
        kf = qk[:, M_QK + h * M_QK_DIM:M_QK + (h + 1) * M_QK_DIM]
        v_ext = jnp.concatenate([v_ref[:, h * M_V_DIM:(h + 1) * M_V_DIM], ones_ext], axis=1)

        scores = (_dot_nt(q, kf.astype(BF16)) * dexp).astype(BF16)
        c_old = c_scr[h]
        nd = _dot(scores, v_ext) + inter_w * _dot(q, c_old.astype(BF16))
        den = nd[:, M_V_DIM:]
        inv = 1.0 / jnp.maximum(jnp.abs(den), jnp.exp(-m_t))
        hs = [nd[:, j * LANES:(j + 1) * LANES] * inv for j in range(M_V_DIM // LANES)]
        hh = jnp.concatenate(hs, axis=1)

        m_new = m_t[L - 1:L, :]
        w_s = jnp.exp(b_last - bcol + icol - m_new)
        decay = jnp.exp(b_last + m_prev - m_new)
        kw = (kf * w_s).astype(BF16)
        c_scr[h] = decay * c_old + _dot_tn(kw, v_ext)
        m_scr[h] = jnp.broadcast_to(m_new, (SUBLANES, LANES))

        hn = hh * lax.rsqrt(jnp.mean(hh * hh, axis=-1, keepdims=True) + EPS)
        sl = slice(h * M_V_DIM, (h + 1) * M_V_DIM)
        og = _sigmoid(o_ref[:, sl].astype(F32))
        y_ref[:, sl] = (og * hn * hg_ref[:, sl]).astype(BF16)


def _mlstm(proj, gates, conv_w, conv_b, head_g, nb, nc):
    t = proj.shape[0]
    L = M_CHUNK
    row = lambda b, c: b * nc + c
    return pl.pallas_call(
        _mlstm_kernel,
        out_shape=jax.ShapeDtypeStruct((t, M_V), BF16),
        grid=(nb, nc),
        in_specs=[
            pl.BlockSpec((L, 2 * M_QK), lambda b, c: (row(b, c), 0)),
            pl.BlockSpec((L, M_V), lambda b, c: (row(b, c), 1)),
            pl.BlockSpec((L, M_V), lambda b, c: (row(b, c), 2)),
            pl.BlockSpec((L, LANES), lambda b, c: (row(b, c), 0)),
            pl.BlockSpec((CONV_W, 2 * M_QK), lambda b, c: (0, 0)),
            pl.BlockSpec((1, 2 * M_QK), lambda b, c: (0, 0)),
            pl.BlockSpec((1, M_V), lambda b, c: (0, 0)),
        ],
        out_specs=pl.BlockSpec((L, M_V), lambda b, c: (row(b, c), 0)),
        scratch_shapes=[
            pltpu.VMEM((SUBLANES + L, 2 * M_QK), F32),
            pltpu.VMEM((M_HEADS, M_QK_DIM, M_V_DIM + LANES), F32),
            pltpu.VMEM((M_HEADS, SUBLANES, LANES), F32),
        ],
        compiler_params=_params("parallel", "arbitrary"),
        name="mlstm",
    )(proj, proj, proj, gates, conv_w, conv_b, head_g)


def _hgrn_kernel(gf_ref, gi_ref, gq_ref, gg_ref, lb_ref, hg_ref, y_ref, st_scr):
    L, SUB, NSUB = G_CHUNK, G_SUB, G_CHUNK // G_SUB

    @pl.when(pl.program_id(1) == 0)
    def _():
        st_scr[...] = jnp.zeros(st_scr.shape, F32)

    lb = lb_ref[...]
    f = lb + (1.0 - lb) * _sigmoid(gf_ref[...].astype(F32))
    k_all = 1.0 - f
    cum_all = jnp.dot(_tril(L).astype(F32), jnp.log(f), precision=HIGHEST,
                      preferred_element_type=F32)
    gq = gq_ref[...].astype(F32)
    q_all = gq * _sigmoid(gq)

    row = lax.broadcasted_iota(jnp.int32, (L, G_DIM), 0)
    row_sub = lax.broadcasted_iota(jnp.int32, (SUB, G_DIM), 0)
    ones_sq = jnp.ones((G_DIM, G_DIM), BF16)

    for h in range(G_HEADS):
        sl = slice(h * G_DIM, (h + 1) * G_DIM)
        cum, q, k = cum_all[:, sl], q_all[:, sl], k_all[:, sl]
        iv = gi_ref[:, sl]
        ivf = iv.astype(F32)
        ends = [cum[SUB * b + SUB - 1:SUB * b + SUB, :] for b in range(NSUB)]
        last = ends[NSUB - 1]
        end_of_row = jnp.concatenate([jnp.broadcast_to(e, (SUB, G_DIM)) for e in ends], axis=0)

        khat = k * jnp.exp(end_of_row - cum)
        qs, ks = [], []
        for b in range(NSUB - 1):
            lo = SUB * (b + 1)
            qs.append(jnp.where(row >= lo, q * jnp.exp(jnp.minimum(cum - ends[b], 0.0)), 0.0))
            ks.append(jnp.where((row >= lo - SUB) & (row < lo), khat, 0.0))
        att = _dot_nt(jnp.concatenate(qs, axis=1).astype(BF16),
                      jnp.concatenate(ks, axis=1).astype(BF16))
        o = _dot(att.astype(BF16), iv)

        st = st_scr[h]
        o = o + _dot_nt((q * jnp.exp(cum)).astype(BF16), st.astype(BF16))

        o_diag = []
        H = SUBLANES
        for a in range(NSUB):
            r0 = SUB * a
            q_a, k_a, c_a, i_a = (z[r0:r0 + SUB, :] for z in (q, k, cum, ivf))
            ms = []
            for s in range(SUB):
                lo = 0 if s < H else H
                diff = c_a[lo:, :] - c_a[s:s + 1, :]
                part = jnp.where(row_sub[lo:lo + H, :] >= s, diff[0:H, :], -jnp.inf)
                diff = jnp.concatenate([part, diff[H:, :]], axis=0) if lo == 0 else part
                ms.append(q_a[lo:, :] * (k_a[s:s + 1, :] * jnp.exp(diff)))
            red = _dot(jnp.concatenate(ms, axis=0).astype(BF16), ones_sq)
            acc_lo = red[0:H, :] * i_a[0:1, :]
            acc_hi = red[H:SUB, :] * i_a[0:1, :]
            for s in range(1, H):
                acc_lo = acc_lo + red[s * SUB:s * SUB + H, :] * i_a[s:s + 1, :]
                acc_hi = acc_hi + red[s * SUB + H:(s + 1) * SUB, :] * i_a[s:s + 1, :]
            for s in range(H, SUB):
                off = H * SUB + (s - H) * H
                acc_hi = acc_hi + red[off:off + H, :] * i_a[s:s + 1, :]
            o_diag += [acc_lo, acc_hi]
        o = o + jnp.concatenate(o_diag, axis=0)

        kd = (k * jnp.exp(last - cum)).astype(BF16)
        st_scr[h] = jnp.exp(last) * st + _dot_tn(iv, kd)

        hn = o * lax.rsqrt(jnp.mean(o * o, axis=-1, keepdims=True) + EPS)
        gg = gg_ref[:, sl].astype(F32)
        y_ref[:, sl] = (hn * hg_ref[:, sl] * (gg * _sigmoid(gg))).astype(BF16)


def _hgrn(proj, lb, head_g, nb, nc, col0):
    t = proj.shape[0]
    L = G_CHUNK
    row = lambda b, c: b * nc + c
    spec = lambda j: pl.BlockSpec((L, G_W), lambda b, c: (row(b, c), col0 + j))
    return pl.pallas_call(
        _hgrn_kernel,
        out_shape=jax.ShapeDtypeStruct((t, G_W), BF16),
        grid=(nb, nc),
        in_specs=[spec(0), spec(1), spec(2), spec(3),
                  pl.BlockSpec((1, G_W), lambda b, c: (0, 0)),
                  pl.BlockSpec((1, G_W), lambda b, c: (0, 0))],
        out_specs=pl.BlockSpec((L, G_W), lambda b, c: (row(b, c), 0)),
        scratch_shapes=[pltpu.VMEM((G_HEADS, G_DIM, G_DIM), F32)],
        compiler_params=_params("parallel", "arbitrary"),
        name="hgrn2",
    )(proj, proj, proj, proj, lb, head_g)


def _merge_kernel(x_ref, ym_ref, yg_ref, am_ref, ag_ref, wm_ref, wg_ref, wo_ref, g2_ref,
                  x1_ref, u2_ref):
    zm = _dot(ym_ref[...], wm_ref[...])
    zg = _dot(yg_ref[...], wg_ref[...])
    z = _sigmoid(am_ref[...].astype(F32)) * zm + _sigmoid(ag_ref[...].astype(F32)) * zg
    x1 = x_ref[...] + _dot(z.astype(BF16), wo_ref[...])
    x1_ref[...] = x1
    u2 = x1 * lax.rsqrt(jnp.mean(x1 * x1, axis=-1, keepdims=True) + EPS) * g2_ref[...]
    u2_ref[...] = u2.astype(BF16)


def _merge(x2, ym, yg, proj, wm, wg, wo, g2, tm, col_am):
    t, d = x2.shape
    rows = lambda w: pl.BlockSpec((tm, w), lambda i: (i, 0))
    full = lambda a: pl.BlockSpec(a.shape, lambda i: (0, 0))
    return pl.pallas_call(
        _merge_kernel,
        out_shape=(jax.ShapeDtypeStruct((t, d), F32), jax.ShapeDtypeStruct((t, d), BF16)),
        grid=(t // tm,),
        in_specs=[rows(d), rows(M_V), rows(G_W),
                  pl.BlockSpec((tm, d), lambda i: (i, col_am)),
                  pl.BlockSpec((tm, d), lambda i: (i, col_am + 1)),
                  full(wm), full(wg), full(wo), full(g2)],
        out_specs=(rows(d), rows(d)),
        compiler_params=_params("parallel"),
        name="merge",
    )(x2, ym, yg, proj, proj, wm, wg, wo, g2)


def _sort16_pairs():
    def merge(lo, hi, r):
        step = r * 2
        if step < hi - lo:
            yield from merge(lo, hi, step)
            yield from merge(lo + r, hi, step)
            yield from [(i, i + r) for i in range(lo + r, hi - r, step)]
        else:
            yield (lo, lo + r)

    def sort(lo, hi):
        if hi - lo >= 1:
            mid = lo + (hi - lo) // 2
            yield from sort(lo, mid)
            yield from sort(mid + 1, hi)
            yield from merge(lo, hi, 1)

    return list(sort(0, P_TOPK - 1))


def _top_rows(vals, k):
    out = []
    for _ in range(k):
        m = jnp.max(vals, axis=0, keepdims=True)
        out.append(m)
        vals = jnp.where(vals == m, -jnp.inf, vals)
    return out


def _top16_of_128(vals):
    K = P_TOPK
    v = [vals[SUBLANES * g:SUBLANES * (g + 1), :] for g in range(K)]
    for a, b in _sort16_pairs():
        v[a], v[b] = jnp.maximum(v[a], v[b]), jnp.minimum(v[a], v[b])
    out = []
    for r in range(K):
        m = jnp.max(v[0], axis=0, keepdims=True)
        out.append(m)
        if r == K - 1:
            break
        hit = v[0] == m
        depth = K - 1 - r
        for i in range(depth):
            v[i] = jnp.where(hit, v[i + 1], v[i])
    return out


def _route_kernel(u2_ref, wq_ref, keys_ref, thr_ref, w0_ref, s1_ref, e1_ref, cand_scr):
    K = P_TOPK
    tb = u2_ref.shape[0]
    q_t = _dot_nt(wq_ref[...], u2_ref[...])
    pairs = [(r, c) for r in range(K) for c in range(K) if (r + 1) * (c + 1) <= K]
    n_pad = cand_scr.shape[0] - len(pairs)
    cand_scr[len(pairs):, :] = jnp.full((n_pad, tb), -jnp.inf, F32)

    for h in range(P_HEADS):
        s = []
        for p in range(2):
            r0 = (2 * h + p) * P_NKEYS
            s.append(_dot(keys_ref[2 * h + p], q_t[r0:r0 + P_NKEYS, :].astype(BF16)))
        a = _top16_of_128(s[0])
        b = _top16_of_128(s[1])
        for n, (r, c) in enumerate(pairs):
            cand_scr[n:n + 1, :] = a[r] + b[c]
        top = _top_rows(cand_scr[...], K)
        z = jnp.ones_like(top[0])
        for r in range(1, K):
            z = z + jnp.exp(top[r] - top[0])
        thr_ref[h] = top[K - 1] - s[0]
        w0_ref[h] = 0.5 * jnp.exp(s[0] - a[0]) / z
        s1_ref[h] = s[1]
        e1_ref[h] = jnp.exp(s[1] - b[0])


def _route(u2, wq_t, keys, tb):
    t, d = u2.shape
    n_pairs = sum(1 for r in range(P_TOPK) for c in range(P_TOPK) if (r + 1) * (c + 1) <= P_TOPK)
    n_cand = -(-n_pairs // SUBLANES) * SUBLANES
    out = jax.ShapeDtypeStruct((P_HEADS, P_NKEYS, t), F32)
    ospec = pl.BlockSpec((P_HEADS, P_NKEYS, tb), lambda i: (0, 0, i))
    return pl.pallas_call(
        _route_kernel,
        out_shape=(out, out, out, out),
        grid=(t // tb,),
        in_specs=[pl.BlockSpec((tb, d), lambda i: (i, 0)),
                  pl.BlockSpec(wq_t.shape, lambda i: (0, 0)),
                  pl.BlockSpec(keys.shape, lambda i: (0, 0, 0))],
        out_specs=(ospec, ospec, ospec, ospec),
        scratch_shapes=[pltpu.VMEM((n_cand, tb), F32)],
        compiler_params=_params("parallel"),
        name="route",
    )(u2, wq_t, keys)


PEER_JB = 16
PEER_IB = 4
PEER_TILES_PER_REGION = 1
MXU_TILE = 256
N_MXU = 2
MXU_ROWS = 64
ACC_ROWS_PER_ENTRY = 4


def _gate_tile(thr_ref, w0_ref, s1_ref, e1_ref, h_in, p_out, ib, tc, jb):
    lanes = slice(tc * LANES, (tc + 1) * LANES)
    jrows = slice(jb * PEER_JB, (jb + 1) * PEER_JB)
    gate = [None] * PEER_IB
    for h in range(P_HEADS):
        s1 = s1_ref[h, jrows, lanes]
        e1 = e1_ref[h, jrows, lanes]
        for k in range(PEER_IB):
            i = ib * PEER_IB + k
            term = (jnp.where(s1 >= thr_ref[h, i:i + 1, lanes], e1, 0.0)
                    * w0_ref[h, i:i + 1, lanes])
            gate[k] = term if gate[k] is None else gate[k] + term
    for k in range(PEER_IB):
        r0 = (ib * PEER_IB + k) * P_NKEYS + jb * PEER_JB
        hv = h_in[r0:r0 + PEER_JB, lanes]
        act = hv * (1.0 + lax.erf(hv * (2.0 ** -0.5)))
        p_out[r0:r0 + PEER_JB, lanes] = (act * gate[k]).astype(BF16)


def _peer_step(u2_ref, u_ref, vt_ref, thr_ref, w0_ref, s1_ref, e1_ref, acc_scr,
               h_in, h_out, p_in, p_out):
    f = pl.program_id(0)
    eb, d = u_ref.shape
    tb = u2_ref.shape[0]
    assert tb == N_MXU * MXU_TILE and eb == d
    n_kt = d // MXU_TILE
    n_c = eb // MXU_ROWS
    per_phase = n_kt * n_c
    tiles = [(ib, tc, jb) for ib in range(eb // (PEER_IB * P_NKEYS))
             for tc in range(tb // LANES) for jb in range(P_NKEYS // PEER_JB)]
    assert len(tiles) % PEER_TILES_PER_REGION == 0
    n_reg = len(tiles) // PEER_TILES_PER_REGION
    assert (2 * per_phase) % n_reg == 0
    per_reg = 2 * per_phase // n_reg

    def weights(w):
        phase, kt = divmod(w, n_kt)
        ks = slice(kt * MXU_TILE, (kt + 1) * MXU_TILE)
        for q in range(N_MXU):
            ts = slice(q * MXU_TILE, (q + 1) * MXU_TILE)
            if phase == 0:
                pltpu.matmul_push_rhs(u2_ref[ts, ks], staging_register=w % 2, mxu_index=q,
                                      transpose=True)
            else:
                pltpu.matmul_push_rhs(p_in[ks, ts], staging_register=w % 2, mxu_index=q)

    def piece(idx):
        w, c = divmod(idx, n_c)
        phase, kt = divmod(w, n_kt)
        rows = slice(c * MXU_ROWS, (c + 1) * MXU_ROWS)
        ks = slice(kt * MXU_TILE, (kt + 1) * MXU_TILE)
        addr = c * (MXU_ROWS // ACC_ROWS_PER_ENTRY)
        if c == n_c // 2 and w + 1 < 2 * n_kt:
            weights(w + 1)
        for q in range(N_MXU):
            ts = slice(q * MXU_TILE, (q + 1) * MXU_TILE)
            if kt == 0 and phase == 0:
                acc_scr[rows, ts] += pltpu.matmul_pop(addr, (MXU_ROWS, MXU_TILE), F32, mxu_index=q)
            if kt == 0 and phase == 1:
                h_out[rows, ts] = pltpu.matmul_pop(addr, (MXU_ROWS, MXU_TILE), F32, mxu_index=q)
            lhs = u_ref[rows, ks] if phase == 0 else vt_ref[rows, ks]
            pltpu.matmul_acc_lhs(addr, lhs, mxu_index=q,
                                 load_staged_rhs=(w % 2) if c == 0 else None)

    weights(0)
    for r in range(n_reg):
        @pl.when(f >= -r)
        def _(r=r):
            for idx in range(r * per_reg, (r + 1) * per_reg):
                piece(idx)
            for tile in tiles[r * PEER_TILES_PER_REGION:(r + 1) * PEER_TILES_PER_REGION]:
                _gate_tile(thr_ref, w0_ref, s1_ref, e1_ref, h_in, p_out, *tile)


def _peer_kernel(n_e, n_steps, u2_ref, x1_ref, fg_ref, u_ref, vt_ref, thr_ref, w0_ref, s1_ref,
                 e1_ref, out_ref, acc_scr, h_scr, p_scr):
    f = pl.program_id(0)
    full = (h_scr.shape[1], MXU_TILE)

    @pl.when(f == 0)
    def _():
        acc_scr[...] = jnp.zeros(acc_scr.shape, F32)
        p_scr[1] = jnp.zeros(p_scr.shape[1:], BF16)
        for q in range(N_MXU):
            h_scr[0, :, q * MXU_TILE:(q + 1) * MXU_TILE] = pltpu.matmul_pop(0, full, F32, mxu_index=q)
        h_scr[1] = jnp.zeros(h_scr.shape[1:], F32)

    step = functools.partial(_peer_step, u2_ref, u_ref, vt_ref, thr_ref, w0_ref, s1_ref, e1_ref,
                             acc_scr)

    @pl.when(f % 2 == 0)
    def _():
        step(h_scr.at[1], h_scr.at[0], p_scr.at[1], p_scr.at[0])

    @pl.when(f % 2 == 1)
    def _():
        step(h_scr.at[0], h_scr.at[1], p_scr.at[0], p_scr.at[1])

    @pl.when((f >= 3) & ((f - 3) % n_e == n_e - 1))
    def _():
        x2 = x1_ref[...] + acc_scr[...].T
        out_ref[...] = (x2 * lax.rsqrt(jnp.mean(x2 * x2, axis=-1, keepdims=True) + EPS)
                        * fg_ref[...])
        acc_scr[...] = jnp.zeros(acc_scr.shape, F32)

    @pl.when(f == n_steps - 1)
    def _():
        for q in range(N_MXU):
            h_scr[0, :, q * MXU_TILE:(q + 1) * MXU_TILE] = pltpu.matmul_pop(0, full, F32, mxu_index=q)


def _peer(u2, x1, fg, u_emb, v_t, thr, w0, s1, e1, tb, eb):
    t, d = u2.shape
    n_e = u_emb.shape[0] // eb
    n_i = eb // P_NKEYS
    n = (t // tb) * n_e
    n_steps = n + 3
    blk = lambda f, lag: jnp.clip(f - lag, 0, n - 1)
    tok = lambda lag: (lambda f: (blk(f, lag) // n_e, 0))
    gate_blk = lambda f: (0, blk(f, 1) % n_e, blk(f, 1) // n_e)
    key_blk = lambda f: (0, 0, blk(f, 1) // n_e)
    return pl.pallas_call(
        functools.partial(_peer_kernel, n_e, n_steps),
        out_shape=jax.ShapeDtypeStruct((t, d), F32),
        grid=(n_steps,),
        in_specs=[pl.BlockSpec((tb, d), tok(0)),
                  pl.BlockSpec((tb, d), tok(3)),
                  pl.BlockSpec((1, d), lambda f: (0, 0)),
                  pl.BlockSpec((eb, d), lambda f: (blk(f, 0) % n_e, 0)),
                  pl.BlockSpec((d, eb), lambda f: (0, blk(f, 2) % n_e)),
                  pl.BlockSpec((P_HEADS, n_i, tb), gate_blk),
                  pl.BlockSpec((P_HEADS, n_i, tb), gate_blk),
                  pl.BlockSpec((P_HEADS, P_NKEYS, tb), key_blk),
                  pl.BlockSpec((P_HEADS, P_NKEYS, tb), key_blk)],
        out_specs=pl.BlockSpec((tb, d), tok(3)),
        scratch_shapes=[pltpu.VMEM((d, tb), F32), pltpu.VMEM((2, eb, tb), F32),
                        pltpu.VMEM((2, eb, tb), BF16)],
        compiler_params=_params("arbitrary"),
        name="peer",
    )(u2, x1, fg, u_emb, v_t, thr, w0, s1, e1)


def _tiles(t):
    pick = lambda pref: next(c for c in (pref, 1024, 512, 256, 128) if c <= pref and t % c == 0)
    return dict(inproj_tm=pick(2048), inproj_tn=1024, merge_tm=pick(512),
                route_tb=pick(256), peer_tb=pick(512), peer_eb=1024)


def kernel(x, norm1_g, w_in, b_in, conv_w, conv_b, m_head_g, lb_table, g_head_g, w_br_m, w_br_g,
           w_out, norm2_g, w_pq, sub_keys, u_emb, v_emb, final_g):
    nb, seq, d = x.shape
    t = nb * seq
    depth = w_in.shape[0]
    assert d == 1024 and seq % M_CHUNK == 0 and t % LANES == 0
    tl = _tiles(t)
    n_gate = 2 * M_HEADS
    g0 = 2 * M_QK + 2 * M_V
    assert (w_in.shape[2] - n_gate) % tl["inproj_tn"] == 0

    lb_all = jnp.cumsum(jax.nn.softmax(lb_table.astype(F32), axis=0), axis=0)
    x2 = x.reshape(t, d)
    row = lambda a: a.reshape(1, -1).astype(F32)

    for l in range(depth):
        w_main = jnp.concatenate([w_in[l][:, :g0], w_in[l][:, g0 + n_gate:]], axis=1).astype(BF16)
        b_main = jnp.concatenate([b_in[l][:g0], b_in[l][g0 + n_gate:]]).reshape(1, -1)
        w_gate = jnp.pad(w_in[l][:, g0:g0 + n_gate], ((0, 0), (0, LANES - n_gate))).astype(BF16)
        b_gate = jnp.pad(b_in[l][g0:g0 + n_gate], (0, LANES - n_gate)).reshape(1, -1)

        proj, gates = _inproj(x2, row(norm1_g[l]), w_main, b_main, w_gate, b_gate,
                              tl["inproj_tm"], tl["inproj_tn"])
        y_m = _mlstm(proj, gates, conv_w[l], row(conv_b[l]), row(m_head_g[l]),
                     nb, seq // M_CHUNK)
        y_g = _hgrn(proj, row(lb_all[l]), row(g_head_g[l]), nb, seq // G_CHUNK, g0 // G_W)
        x1, u2 = _merge(x2, y_m, y_g, proj, w_br_m[l].astype(BF16), w_br_g[l].astype(BF16),
                        w_out[l].astype(BF16), row(norm2_g[l]), tl["merge_tm"],
                        (g0 + 4 * G_W) // d)

        keys = sub_keys[l].reshape(2 * P_HEADS, P_NKEYS, -1).astype(BF16)
        thr, w0, s1, e1 = _route(u2, w_pq[l].T.astype(BF16), keys, tl["route_tb"])
        assert depth == 1
        x2 = _peer(u2, x1, row(final_g), u_emb[l].astype(BF16), v_emb[l].T.astype(BF16),
                   thr, w0, s1, e1, tl["peer_tb"], tl["peer_eb"])
    return x2.reshape(nb, seq, d)
```

```python
import functools

import jax
import jax.numpy as jnp
from jax import lax
from jax.experimental import pallas as pl
from jax.experimental.pallas import tpu as pltpu

F32 = jnp.float32
BF16 = jnp.bfloat16
HIGHEST = lax.Precision.HIGHEST

EPS = 1e-6
LOG2E = 1.4426950408889634
LANES = 128
SUBLANES = 8
VMEM_LIMIT = 48 * 1024 * 1024

M_HEADS, M_QK_DIM, M_V_DIM, M_CHUNK, CONV_W = 4, 256, 512, 128, 4
G_HEADS, G_DIM, G_CHUNK, G_SUB = 8, 128, 64, 16
P_HEADS, P_NKEYS, P_TOPK = 8, 128, 16
M_QK = M_HEADS * M_QK_DIM
M_V = M_HEADS * M_V_DIM
G_W = G_HEADS * G_DIM


def _params(*sem):
    return pltpu.CompilerParams(dimension_semantics=sem, vmem_limit_bytes=VMEM_LIMIT)


def _dot(a, b):
    return jnp.dot(a, b, preferred_element_type=F32)


def _dot_nt(a, b):
    return lax.dot_general(a, b, (((1,), (1,)), ((), ())), preferred_element_type=F32)


def _dot_tn(a, b):
    return lax.dot_general(a, b, (((0,), (0,)), ((), ())), preferred_element_type=F32)


def _sigmoid(x):
    return 1.0 / (1.0 + jnp.exp2(x * (-LOG2E)))


def _tril(n):
    r = lax.broadcasted_iota(jnp.int32, (n, n), 0)
    c = lax.broadcasted_iota(jnp.int32, (n, n), 1)
    return (r >= c)


def _inproj_kernel(x_ref, g_ref, w_ref, b_ref, wg_ref, bg_ref, out_ref, gate_ref, u_scr):
    @pl.when(pl.program_id(1) == 0)
    def _():
        x = x_ref[...]
        u = x * lax.rsqrt(jnp.mean(x * x, axis=-1, keepdims=True) + EPS) * g_ref[...]
        u_scr[...] = u.astype(BF16)
        gate_ref[...] = _dot(u.astype(BF16), wg_ref[...]) + bg_ref[...]

    out_ref[...] = (_dot(u_scr[...], w_ref[...]) + b_ref[...]).astype(BF16)


def _inproj(x2, g, w, b, wg, bg, tm, tn):
    t, d = x2.shape
    n = w.shape[1]
    return pl.pallas_call(
        _inproj_kernel,
        out_shape=(jax.ShapeDtypeStruct((t, n), BF16), jax.ShapeDtypeStruct((t, LANES), F32)),
        grid=(t // tm, n // tn),
        in_specs=[
            pl.BlockSpec((tm, d), lambda i, j: (i, 0)),
            pl.BlockSpec((1, d), lambda i, j: (0, 0)),
            pl.BlockSpec((d, tn), lambda i, j: (0, j)),
            pl.BlockSpec((1, tn), lambda i, j: (0, j)),
            pl.BlockSpec((d, LANES), lambda i, j: (0, 0)),
            pl.BlockSpec((1, LANES), lambda i, j: (0, 0)),
        ],
        out_specs=(pl.BlockSpec((tm, tn), lambda i, j: (i, j)),
                   pl.BlockSpec((tm, LANES), lambda i, j: (i, 0))),
        scratch_shapes=[pltpu.VMEM((tm, d), BF16)],
        compiler_params=_params("parallel", "arbitrary"),
        name="inproj",
    )(x2, g, w, b, wg, bg)


def _mlstm_kernel(qk_ref, v_ref, o_ref, gt_ref, cw_ref, cb_ref, hg_ref, y_ref,
                  ext_scr, c_scr, m_scr):
    L = M_CHUNK

    @pl.when(pl.program_id(1) == 0)
    def _():
        ext_scr[0:SUBLANES, :] = jnp.zeros((SUBLANES, 2 * M_QK), F32)
        c_scr[...] = jnp.zeros(c_scr.shape, F32)
        m_scr[...] = jnp.zeros(m_scr.shape, F32)

    cur = qk_ref[...].astype(F32)
    ext_scr[SUBLANES:SUBLANES + L, :] = cur
    acc = cb_ref[...] + cw_ref[CONV_W - 1:CONV_W, :] * cur
    for j in range(CONV_W - 1):
        off = SUBLANES - (CONV_W - 1) + j
        acc = acc + cw_ref[j:j + 1, :] * ext_scr[off:off + L, :]
    ext_scr[0:SUBLANES, :] = cur[L - SUBLANES:L, :]
    qk = acc * _sigmoid(acc)

    gt = gt_ref[...]
    lf = jnp.minimum(gt, 0.0) - jnp.log(1.0 + jnp.exp(-jnp.abs(gt)))
    tri = _tril(L)
    bcols = jnp.dot(tri.astype(F32), lf, precision=HIGHEST, preferred_element_type=F32)
    gt_t = gt.T
    b_t = bcols.T
    ones_ext = jnp.ones((L, LANES), BF16)

    for h in range(M_HEADS):
        icol = gt[:, h:h + 1]
        bcol = bcols[:, M_HEADS + h:M_HEADS + h + 1]
        irow = gt_t[h:h + 1, :]
        brow = b_t[M_HEADS + h:M_HEADS + h + 1, :]
        m_prev = m_scr[h, 0:1, 0:1]
        b_last = bcol[L - 1:L, :]

        dmat = jnp.where(tri, bcol - brow + irow, -jnp.inf)
        inter = bcol + m_prev
        m_t = jnp.maximum(inter, jnp.max(dmat, axis=-1, keepdims=True))
        dexp = jnp.exp(dmat - m_t)
        inter_w = jnp.exp(inter - m_t)

        q = (qk[:, h * M_QK_DIM:(h + 1) * M_QK_DIM] * (M_QK_DIM ** -0.5)).astype(BF16)
        kf = qk[:, M_QK + h * M_QK_DIM:M_QK + (h + 1) * M_QK_DIM]
        v_ext = jnp.concatenate([v_ref[:, h * M_V_DIM:(h + 1) * M_V_DIM], ones_ext], axis=1)

        scores = (_dot_nt(q, kf.astype(BF16)) * dexp).astype(BF16)
        c_old = c_scr[h]
        nd = _dot(scores, v_ext) + inter_w * _dot(q, c_old.astype(BF16))
        den = nd[:, M_V_DIM:]
        inv = 1.0 / jnp.maximum(jnp.abs(den), jnp.exp(-m_t))
        hs = [nd[:, j * LANES:(j + 1) * LANES] * inv for j in range(M_V_DIM // LANES)]
        hh = jnp.concatenate(hs, axis=1)

        m_new = m_t[L - 1:L, :]
        w_s = jnp.exp(b_last - bcol + icol - m_new)
        decay = jnp.exp(b_last + m_prev - m_new)
        kw = (kf * w_s).astype(BF16)
        c_scr[h] = decay * c_old + _dot_tn(kw, v_ext)
        m_scr[h] = jnp.broadcast_to(m_new, (SUBLANES, LANES))

        hn = hh * lax.rsqrt(jnp.mean(hh * hh, axis=-1, keepdims=True) + EPS)
        sl = slice(h * M_V_DIM, (h + 1) * M_V_DIM)
        og = _sigmoid(o_ref[:, sl].astype(F32))
        y_ref[:, sl] = (og * hn * hg_ref[:, sl]).astype(BF16)


def _mlstm(proj, gates, conv_w, conv_b, head_g, nb, nc):
    t = proj.shape[0]
    L = M_CHUNK
    row = lambda b, c: b * nc + c
    return pl.pallas_call(
        _mlstm_kernel,
        out_shape=jax.ShapeDtypeStruct((t, M_V), BF16),
        grid=(nb, nc),
        in_specs=[
            pl.BlockSpec((L, 2 * M_QK), lambda b, c: (row(b, c), 0)),
            pl.BlockSpec((L, M_V), lambda b, c: (row(b, c), 1)),
            pl.BlockSpec((L, M_V), lambda b, c: (row(b, c), 2)),
            pl.BlockSpec((L, LANES), lambda b, c: (row(b, c), 0)),
            pl.BlockSpec((CONV_W, 2 * M_QK), lambda b, c: (0, 0)),
            pl.BlockSpec((1, 2 * M_QK), lambda b, c: (0, 0)),
            pl.BlockSpec((1, M_V), lambda b, c: (0, 0)),
        ],
        out_specs=pl.BlockSpec((L, M_V), lambda b, c: (row(b, c), 0)),
        scratch_shapes=[
            pltpu.VMEM((SUBLANES + L, 2 * M_QK), F32),
            pltpu.VMEM((M_HEADS, M_QK_DIM, M_V_DIM + LANES), F32),
            pltpu.VMEM((M_HEADS, SUBLANES, LANES), F32),
        ],
        compiler_params=_params("parallel", "arbitrary"),
        name="mlstm",
    )(proj, proj, proj, gates, conv_w, conv_b, head_g)


def _hgrn_kernel(gf_ref, gi_ref, gq_ref, gg_ref, lb_ref, hg_ref, y_ref, st_scr):
    L, SUB, NSUB = G_CHUNK, G_SUB, G_CHUNK // G_SUB

    @pl.when(pl.program_id(1) == 0)
    def _():
        st_scr[...] = jnp.zeros(st_scr.shape, F32)

    lb = lb_ref[...]
    f = lb + (1.0 - lb) * _sigmoid(gf_ref[...].astype(F32))
    k_all = 1.0 - f
    cum_all = jnp.dot(_tril(L).astype(F32), jnp.log(f), precision=HIGHEST,
                      preferred_element_type=F32)
    gq = gq_ref[...].astype(F32)
    q_all = gq * _sigmoid(gq)

    row = lax.broadcasted_iota(jnp.int32, (L, G_DIM), 0)
    row_sub = lax.broadcasted_iota(jnp.int32, (SUB, G_DIM), 0)
    ones_sq = jnp.ones((G_DIM, G_DIM), BF16)

    for h in range(G_HEADS):
        sl = slice(h * G_DIM, (h + 1) * G_DIM)
        cum, q, k = cum_all[:, sl], q_all[:, sl], k_all[:, sl]
        iv = gi_ref[:, sl]
        ivf = iv.astype(F32)
        ends = [cum[SUB * b + SUB - 1:SUB * b + SUB, :] for b in range(NSUB)]
        last = ends[NSUB - 1]
        end_of_row = jnp.concatenate([jnp.broadcast_to(e, (SUB, G_DIM)) for e in ends], axis=0)

        khat = k * jnp.exp(end_of_row - cum)
        qs, ks = [], []
        for b in range(NSUB - 1):
            lo = SUB * (b + 1)
            qs.append(jnp.where(row >= lo, q * jnp.exp(jnp.minimum(cum - ends[b], 0.0)), 0.0))
            ks.append(jnp.where((row >= lo - SUB) & (row < lo), khat, 0.0))
        att = _dot_nt(jnp.concatenate(qs, axis=1).astype(BF16),
                      jnp.concatenate(ks, axis=1).astype(BF16))
        o = _dot(att.astype(BF16), iv)

        st = st_scr[h]
        o = o + _dot_nt((q * jnp.exp(cum)).astype(BF16), st.astype(BF16))

        o_diag = []
        H = SUBLANES
        for a in range(NSUB):
            r0 = SUB * a
            q_a, k_a, c_a, i_a = (z[r0:r0 + SUB, :] for z in (q, k, cum, ivf))
            ms = []
            for s in range(SUB):
                lo = 0 if s < H else H
                diff = c_a[lo:, :] - c_a[s:s + 1, :]
                part = jnp.where(row_sub[lo:lo + H, :] >= s, diff[0:H, :], -jnp.inf)
                diff = jnp.concatenate([part, diff[H:, :]], axis=0) if lo == 0 else part
                ms.append(q_a[lo:, :] * (k_a[s:s + 1, :] * jnp.exp(diff)))
            red = _dot(jnp.concatenate(ms, axis=0).astype(BF16), ones_sq)
            acc_lo = red[0:H, :] * i_a[0:1, :]
            acc_hi = red[H:SUB, :] * i_a[0:1, :]
            for s in range(1, H):
                acc_lo = acc_lo + red[s * SUB:s * SUB + H, :] * i_a[s:s + 1, :]
                acc_hi = acc_hi + red[s * SUB + H:(s + 1) * SUB, :] * i_a[s:s + 1, :]
            for s in range(H, SUB):
                off = H * SUB + (s - H) * H
                acc_hi = acc_hi + red[off:off + H, :] * i_a[s:s + 1, :]
            o_diag += [acc_lo, acc_hi]
        o = o + jnp.concatenate(o_diag, axis=0)

        kd = (k * jnp.exp(last - cum)).astype(BF16)
        st_scr[h] = jnp.exp(last) * st + _dot_tn(iv, kd)

        hn = o * lax.rsqrt(jnp.mean(o * o, axis=-1, keepdims=True) + EPS)
        gg = gg_ref[:, sl].astype(F32)
        y_ref[:, sl] = (hn * hg_ref[:, sl] * (gg * _sigmoid(gg))).astype(BF16)


def _hgrn(proj, lb, head_g, nb, nc, col0):
    t = proj.shape[0]
    L = G_CHUNK
    row = lambda b, c: b * nc + c
    spec = lambda j: pl.BlockSpec((L, G_W), lambda b, c: (row(b, c), col0 + j))
    return pl.pallas_call(
        _hgrn_kernel,
        out_shape=jax.ShapeDtypeStruct((t, G_W), BF16),
        grid=(nb, nc),
        in_specs=[spec(0), spec(1), spec(2), spec(3),
                  pl.BlockSpec((1, G_W), lambda b, c: (0, 0)),
                  pl.BlockSpec((1, G_W), lambda b, c: (0, 0))],
        out_specs=pl.BlockSpec((L, G_W), lambda b, c: (row(b, c), 0)),
        scratch_shapes=[pltpu.VMEM((G_HEADS, G_DIM, G_DIM), F32)],
        compiler_params=_params("parallel", "arbitrary"),
        name="hgrn2",
    )(proj, proj, proj, proj, lb, head_g)


def _merge_kernel(x_ref, ym_ref, yg_ref, am_ref, ag_ref, wm_ref, wg_ref, wo_ref, g2_ref,
                  x1_ref, u2_ref):
    zm = _dot(ym_ref[...], wm_ref[...])
    zg = _dot(yg_ref[...], wg_ref[...])
    z = _sigmoid(am_ref[...].astype(F32)) * zm + _sigmoid(ag_ref[...].astype(F32)) * zg
    x1 = x_ref[...] + _dot(z.astype(BF16), wo_ref[...])
    x1_ref[...] = x1
    u2 = x1 * lax.rsqrt(jnp.mean(x1 * x1, axis=-1, keepdims=True) + EPS) * g2_ref[...]
    u2_ref[...] = u2.astype(BF16)


def _merge(x2, ym, yg, proj, wm, wg, wo, g2, tm, col_am):
    t, d = x2.shape
    rows = lambda w: pl.BlockSpec((tm, w), lambda i: (i, 0))
    full = lambda a: pl.BlockSpec(a.shape, lambda i: (0, 0))
    return pl.pallas_call(
        _merge_kernel,
        out_shape=(jax.ShapeDtypeStruct((t, d), F32), jax.ShapeDtypeStruct((t, d), BF16)),
        grid=(t // tm,),
        in_specs=[rows(d), rows(M_V), rows(G_W),
                  pl.BlockSpec((tm, d), lambda i: (i, col_am)),
                  pl.BlockSpec((tm, d), lambda i: (i, col_am + 1)),
                  full(wm), full(wg), full(wo), full(g2)],
        out_specs=(rows(d), rows(d)),
        compiler_params=_params("parallel"),
        name="merge",
    )(x2, ym, yg, proj, proj, wm, wg, wo, g2)


def _sort16_pairs():
    def merge(lo, hi, r):
        step = r * 2
        if step < hi - lo:
            yield from merge(lo, hi, step)
            yield from merge(lo + r, hi, step)
            yield from [(i, i + r) for i in range(lo + r, hi - r, step)]
        else:
            yield (lo, lo + r)

    def sort(lo, hi):
        if hi - lo >= 1:
            mid = lo + (hi - lo) // 2
            yield from sort(lo, mid)
            yield from sort(mid + 1, hi)
            yield from merge(lo, hi, 1)

    return list(sort(0, P_TOPK - 1))


def _top_rows(vals, k):
    out = []
    for _ in range(k):
        m = jnp.max(vals, axis=0, keepdims=True)
        out.append(m)
        vals = jnp.where(vals == m, -jnp.inf, vals)
    return out


def _top_of_128(vals, k):
    n = P_TOPK
    v = [vals[SUBLANES * g:SUBLANES * (g + 1), :] for g in range(n)]
    for a, b in _sort16_pairs():
        v[a], v[b] = jnp.maximum(v[a], v[b]), jnp.minimum(v[a], v[b])
    out = []
    for r in range(k):
        m = jnp.max(v[0], axis=0, keepdims=True)
        out.append(m)
        if r == k - 1:
            break
        hit = v[0] == m
        depth = k - 1 - r
        for i in range(min(depth, n - 1)):
            v[i] = jnp.where(hit, v[i + 1], v[i])
        if depth >= n:
            v[n - 1] = jnp.where(hit, -jnp.inf, v[n - 1])
    return out


def _route_kernel(u2_ref, wq_ref, keys_ref, thr_ref, w0_ref, s1_ref, e1_ref, cand_scr):
    K = P_TOPK
    tb = u2_ref.shape[0]
    q_t = _dot_nt(wq_ref[...], u2_ref[...])
    pairs = [(r, c) for r in range(K + 1) for c in range(K + 1) if (r + 1) * (c + 1) <= K + 1]
    n_pad = cand_scr.shape[0] - len(pairs)
    cand_scr[len(pairs):, :] = jnp.full((n_pad, tb), -jnp.inf, F32)

    for h in range(P_HEADS):
        s = []
        for p in range(2):
            r0 = (2 * h + p) * P_NKEYS
            s.append(_dot(keys_ref[2 * h + p], q_t[r0:r0 + P_NKEYS, :].astype(BF16)))
        a = _top_of_128(s[0], K + 1)
        b = _top_of_128(s[1], K + 1)
        for n, (r, c) in enumerate(pairs):
            cand_scr[n:n + 1, :] = a[r] + b[c]
        top = _top_rows(cand_scr[...], K + 1)
        z = jnp.ones_like(top[0])
        for r in range(1, K):
            z = z + jnp.exp(top[r] - top[0])
        tau = 0.5 * (top[K - 1] + top[K])
        thr_ref[h] = tau - s[0]
        w0_ref[h] = 0.5 * jnp.exp(s[0] - a[0]) / z
        s1_ref[h] = s[1]
        e1_ref[h] = jnp.exp(s[1] - b[0])


def _route(u2, wq_t, keys, tb):
    t, d = u2.shape
    n_pairs = sum(1 for r in range(P_TOPK + 1) for c in range(P_TOPK + 1)
                  if (r + 1) * (c + 1) <= P_TOPK + 1)
    n_cand = -(-n_pairs // SUBLANES) * SUBLANES
    out = jax.ShapeDtypeStruct((P_HEADS, P_NKEYS, t), F32)
    ospec = pl.BlockSpec((P_HEADS, P_NKEYS, tb), lambda i: (0, 0, i))
    return pl.pallas_call(
        _route_kernel,
        out_shape=(out, out, out, out),
        grid=(t // tb,),
        in_specs=[pl.BlockSpec((tb, d), lambda i: (i, 0)),
                  pl.BlockSpec(wq_t.shape, lambda i: (0, 0)),
                  pl.BlockSpec(keys.shape, lambda i: (0, 0, 0))],
        out_specs=(ospec, ospec, ospec, ospec),
        scratch_shapes=[pltpu.VMEM((n_cand, tb), F32)],
        compiler_params=_params("parallel"),
        name="route",
    )(u2, wq_t, keys)


PEER_JB = 32
PEER_IB = 4
PEER_TILES_PER_REGION = 2
MXU_TILE = 256
N_MXU = 2
MXU_ROWS = 64
ACC_ROWS_PER_ENTRY = 4


def _gate_tile(thr_ref, w0_ref, s1_ref, e1_ref, h_in, p_out, ib, tc, jb):
    lanes = slice(tc * LANES, (tc + 1) * LANES)
    jrows = slice(jb * PEER_JB, (jb + 1) * PEER_JB)
    gate = [None] * PEER_IB
    for h in range(P_HEADS):
        s1 = s1_ref[h, jrows, lanes]
        e1 = e1_ref[h, jrows, lanes]
        for k in range(PEER_IB):
            i = ib * PEER_IB + k
            term = (jnp.where(s1 >= thr_ref[h, i:i + 1, lanes], e1, 0.0)
                    * w0_ref[h, i:i + 1, lanes])
            gate[k] = term if gate[k] is None else gate[k] + term
    for k in range(PEER_IB):
        r0 = (ib * PEER_IB + k) * P_NKEYS + jb * PEER_JB
        hv = h_in[r0:r0 + PEER_JB, lanes]
        act = hv * (1.0 + lax.erf(hv * (2.0 ** -0.5)))
        p_out[r0:r0 + PEER_JB, lanes] = (act * gate[k]).astype(BF16)


def _peer_step(u2_ref, u_ref, vt_ref, thr_ref, w0_ref, s1_ref, e1_ref, acc_scr,
               h_in, h_out, p_in, p_out):
    f = pl.program_id(0)
    eb, d = u_ref.shape
    tb = u2_ref.shape[0]
    assert tb == N_MXU * MXU_TILE and eb == d
    n_kt = d // MXU_TILE
    n_c = eb // MXU_ROWS
    per_phase = n_kt * n_c
    tiles = [(ib, tc, jb) for ib in range(eb // (PEER_IB * P_NKEYS))
             for tc in range(tb // LANES) for jb in range(P_NKEYS // PEER_JB)]
    assert len(tiles) % PEER_TILES_PER_REGION == 0
    n_reg = len(tiles) // PEER_TILES_PER_REGION
    assert (2 * per_phase) % n_reg == 0
    per_reg = 2 * per_phase // n_reg

    def weights(w):
        phase, kt = divmod(w, n_kt)
        ks = slice(kt * MXU_TILE, (kt + 1) * MXU_TILE)
        for q in range(N_MXU):
            ts = slice(q * MXU_TILE, (q + 1) * MXU_TILE)
            if phase == 0:
                pltpu.matmul_push_rhs(u2_ref[ts, ks], staging_register=w % 2, mxu_index=q,
                                      transpose=True)
            else:
                pltpu.matmul_push_rhs(p_in[ks, ts], staging_register=w % 2, mxu_index=q)

    def piece(idx):
        w, c = divmod(idx, n_c)
        phase, kt = divmod(w, n_kt)
        rows = slice(c * MXU_ROWS, (c + 1) * MXU_ROWS)
        ks = slice(kt * MXU_TILE, (kt + 1) * MXU_TILE)
        addr = c * (MXU_ROWS // ACC_ROWS_PER_ENTRY)
        if c == n_c // 2 and w + 1 < 2 * n_kt:
            weights(w + 1)
        for q in range(N_MXU):
            ts = slice(q * MXU_TILE, (q + 1) * MXU_TILE)
            if kt == 0 and phase == 0:
                acc_scr[rows, ts] += pltpu.matmul_pop(addr, (MXU_ROWS, MXU_TILE), F32, mxu_index=q)
            if kt == 0 and phase == 1:
                h_out[rows, ts] = pltpu.matmul_pop(addr, (MXU_ROWS, MXU_TILE), F32, mxu_index=q)
            lhs = u_ref[rows, ks] if phase == 0 else vt_ref[rows, ks]
            pltpu.matmul_acc_lhs(addr, lhs, mxu_index=q,
                                 load_staged_rhs=(w % 2) if c == 0 else None)

    weights(0)
    for r in range(n_reg):
        @pl.when(f >= -r)
        def _(r=r):
            for idx in range(r * per_reg, (r + 1) * per_reg):
                piece(idx)
            for tile in tiles[r * PEER_TILES_PER_REGION:(r + 1) * PEER_TILES_PER_REGION]:
                _gate_tile(thr_ref, w0_ref, s1_ref, e1_ref, h_in, p_out, *tile)


def _peer_kernel(n_e, n_steps, u2_ref, x1_ref, fg_ref, u_ref, vt_ref, thr_ref, w0_ref, s1_ref,
                 e1_ref, out_ref, acc_scr, h_scr, p_scr):
    f = pl.program_id(0)
    full = (h_scr.shape[1], MXU_TILE)

    @pl.when(f == 0)
    def _():
        acc_scr[...] = jnp.zeros(acc_scr.shape, F32)
        p_scr[1] = jnp.zeros(p_scr.shape[1:], BF16)
        for q in range(N_MXU):
            h_scr[0, :, q * MXU_TILE:(q + 1) * MXU_TILE] = pltpu.matmul_pop(0, full, F32, mxu_index=q)
        h_scr[1] = jnp.zeros(h_scr.shape[1:], F32)

    step = functools.partial(_peer_step, u2_ref, u_ref, vt_ref, thr_ref, w0_ref, s1_ref, e1_ref,
                             acc_scr)

    @pl.when(f % 2 == 0)
    def _():
        step(h_scr.at[1], h_scr.at[0], p_scr.at[1], p_scr.at[0])

    @pl.when(f % 2 == 1)
    def _():
        step(h_scr.at[0], h_scr.at[1], p_scr.at[0], p_scr.at[1])

    @pl.when((f >= 3) & ((f - 3) % n_e == n_e - 1))
    def _():
        x2 = x1_ref[...] + acc_scr[...].T
        out_ref[...] = (x2 * lax.rsqrt(jnp.mean(x2 * x2, axis=-1, keepdims=True) + EPS)
                        * fg_ref[...])
        acc_scr[...] = jnp.zeros(acc_scr.shape, F32)

    @pl.when(f == n_steps - 1)
    def _():
        for q in range(N_MXU):
            h_scr[0, :, q * MXU_TILE:(q + 1) * MXU_TILE] = pltpu.matmul_pop(0, full, F32, mxu_index=q)


def _peer(u2, x1, fg, u_emb, v_t, thr, w0, s1, e1, tb, eb):
    t, d = u2.shape
    n_e = u_emb.shape[0] // eb
    n_i = eb // P_NKEYS
    n = (t // tb) * n_e
    n_steps = n + 3
    blk = lambda f, lag: jnp.clip(f - lag, 0, n - 1)
    tok = lambda lag: (lambda f: (blk(f, lag) // n_e, 0))
    gate_blk = lambda f: (0, blk(f, 1) % n_e, blk(f, 1) // n_e)
    key_blk = lambda f: (0, 0, blk(f, 1) // n_e)
    return pl.pallas_call(
        functools.partial(_peer_kernel, n_e, n_steps),
        out_shape=jax.ShapeDtypeStruct((t, d), F32),
        grid=(n_steps,),
        in_specs=[pl.BlockSpec((tb, d), tok(0)),
                  pl.BlockSpec((tb, d), tok(3)),
                  pl.BlockSpec((1, d), lambda f: (0, 0)),
                  pl.BlockSpec((eb, d), lambda f: (blk(f, 0) % n_e, 0)),
                  pl.BlockSpec((d, eb), lambda f: (0, blk(f, 2) % n_e)),
                  pl.BlockSpec((P_HEADS, n_i, tb), gate_blk),
                  pl.BlockSpec((P_HEADS, n_i, tb), gate_blk),
                  pl.BlockSpec((P_HEADS, P_NKEYS, tb), key_blk),
                  pl.BlockSpec((P_HEADS, P_NKEYS, tb), key_blk)],
        out_specs=pl.BlockSpec((tb, d), tok(3)),
        scratch_shapes=[pltpu.VMEM((d, tb), F32), pltpu.VMEM((2, eb, tb), F32),
                        pltpu.VMEM((2, eb, tb), BF16)],
        compiler_params=_params("arbitrary"),
        name="peer",
    )(u2, x1, fg, u_emb, v_t, thr, w0, s1, e1)


def _tiles(t):
    pick = lambda pref: next(c for c in (pref, 1024, 512, 256, 128) if c <= pref and t % c == 0)
    return dict(inproj_tm=pick(2048), inproj_tn=1024, merge_tm=pick(512),
                route_tb=pick(256), peer_tb=pick(512), peer_eb=1024)


def kernel(x, norm1_g, w_in, b_in, conv_w, conv_b, m_head_g, lb_table, g_head_g, w_br_m, w_br_g,
           w_out, norm2_g, w_pq, sub_keys, u_emb, v_emb, final_g):
    nb, seq, d = x.shape
    t = nb * seq
    depth = w_in.shape[0]
    assert d == 1024 and seq % M_CHUNK == 0 and t % LANES == 0
    tl = _tiles(t)
    n_gate = 2 * M_HEADS
    g0 = 2 * M_QK + 2 * M_V
    assert (w_in.shape[2] - n_gate) % tl["inproj_tn"] == 0

    lb_all = jnp.cumsum(jax.nn.softmax(lb_table.astype(F32), axis=0), axis=0)
    x2 = x.reshape(t, d)
    row = lambda a: a.reshape(1, -1).astype(F32)

    for l in range(depth):
        w_main = jnp.concatenate([w_in[l][:, :g0], w_in[l][:, g0 + n_gate:]], axis=1).astype(BF16)
        b_main = jnp.concatenate([b_in[l][:g0], b_in[l][g0 + n_gate:]]).reshape(1, -1)
        w_gate = jnp.pad(w_in[l][:, g0:g0 + n_gate], ((0, 0), (0, LANES - n_gate))).astype(BF16)
        b_gate = jnp.pad(b_in[l][g0:g0 + n_gate], (0, LANES - n_gate)).reshape(1, -1)

        proj, gates = _inproj(x2, row(norm1_g[l]), w_main, b_main, w_gate, b_gate,
                              tl["inproj_tm"], tl["inproj_tn"])
        y_m = _mlstm(proj, gates, conv_w[l], row(conv_b[l]), row(m_head_g[l]),
                     nb, seq // M_CHUNK)
        y_g = _hgrn(proj, row(lb_all[l]), row(g_head_g[l]), nb, seq // G_CHUNK, g0 // G_W)
        x1, u2 = _merge(x2, y_m, y_g, proj, w_br_m[l].astype(BF16), w_br_g[l].astype(BF16),
                        w_out[l].astype(BF16), row(norm2_g[l]), tl["merge_tm"],
                        (g0 + 4 * G_W) // d)

        keys = sub_keys[l].reshape(2 * P_HEADS, P_NKEYS, -1).astype(BF16)
        thr, w0, s1, e1 = _route(u2, w_pq[l].T.astype(BF16), keys, tl["route_tb"])
        assert depth == 1
        x2 = _peer(u2, x1, row(final_g), u_emb[l].astype(BF16), v_emb[l].T.astype(BF16),
                   thr, w0, s1, e1, tl["peer_tb"], tl["peer_eb"])
    return x2.reshape(nb, seq, d)
```

```python
import functools

import jax
import jax.numpy as jnp
from jax import lax
from jax.experimental import pallas as pl
from jax.experimental.pallas import tpu as pltpu

F32 = jnp.float32
BF16 = jnp.bfloat16
HIGHEST = lax.Precision.HIGHEST

EPS = 1e-6
LOG2E = 1.4426950408889634
LANES = 128
SUBLANES = 8
VMEM_LIMIT = 48 * 1024 * 1024

M_HEADS, M_QK_DIM, M_V_DIM, M_CHUNK, CONV_W = 4, 256, 512, 128, 4
M_CHUNKS_PER_STEP = 2
G_HEADS, G_DIM, G_CHUNK, G_SUB = 8, 128, 64, 16
G_CHUNKS_PER_STEP = 4
P_HEADS, P_NKEYS, P_TOPK = 8, 128, 16
M_QK = M_HEADS * M_QK_DIM
M_V = M_HEADS * M_V_DIM
G_W = G_HEADS * G_DIM


def _params(*sem):
    return pltpu.CompilerParams(dimension_semantics=sem, vmem_limit_bytes=VMEM_LIMIT)


def _dot(a, b):
    return jnp.dot(a, b, preferred_element_type=F32)


def _dot_nt(a, b):
    return lax.dot_general(a, b, (((1,), (1,)), ((), ())), preferred_element_type=F32)


def _dot_tn(a, b):
    return lax.dot_general(a, b, (((0,), (0,)), ((), ())), preferred_element_type=F32)


def _sigmoid(x):
    return 1.0 / (1.0 + jnp.exp2(x * (-LOG2E)))


def _tril(n):
    r = lax.broadcasted_iota(jnp.int32, (n, n), 0)
    c = lax.broadcasted_iota(jnp.int32, (n, n), 1)
    return (r >= c)


def _inproj_kernel(x_ref, g_ref, w_ref, b_ref, wg_ref, bg_ref, out_ref, gate_ref, u_scr):
    @pl.when(pl.program_id(1) == 0)
    def _():
        x = x_ref[...]
        u = x * lax.rsqrt(jnp.mean(x * x, axis=-1, keepdims=True) + EPS) * g_ref[...]
        u_scr[...] = u.astype(BF16)
        gate_ref[...] = _dot(u.astype(BF16), wg_ref[...]) + bg_ref[...]

    out_ref[...] = (_dot(u_scr[...], w_ref[...]) + b_ref[...]).astype(BF16)


def _inproj(x2, g, w, b, wg, bg, tm, tn):
    t, d = x2.shape
    n = w.shape[1]
    return pl.pallas_call(
        _inproj_kernel,
        out_shape=(jax.ShapeDtypeStruct((t, n), BF16), jax.ShapeDtypeStruct((t, LANES), F32)),
        grid=(t // tm, n // tn),
        in_specs=[
            pl.BlockSpec((tm, d), lambda i, j: (i, 0)),
            pl.BlockSpec((1, d), lambda i, j: (0, 0)),
            pl.BlockSpec((d, tn), lambda i, j: (0, j)),
            pl.BlockSpec((1, tn), lambda i, j: (0, j)),
            pl.BlockSpec((d, LANES), lambda i, j: (0, 0)),
            pl.BlockSpec((1, LANES), lambda i, j: (0, 0)),
        ],
        out_specs=(pl.BlockSpec((tm, tn), lambda i, j: (i, j)),
                   pl.BlockSpec((tm, LANES), lambda i, j: (i, 0))),
        scratch_shapes=[pltpu.VMEM((tm, d), BF16)],
        compiler_params=_params("parallel", "arbitrary"),
        name="inproj",
    )(x2, g, w, b, wg, bg)


def _mlstm_kernel(qk_ref, v_ref, o_ref, gt_ref, cw_ref, cb_ref, hg_ref, y_ref,
                  ext_scr, c_scr, m_scr):
    L = M_CHUNK

    @pl.when(pl.program_id(1) == 0)
    def _():
        ext_scr[0:SUBLANES, :] = jnp.zeros((SUBLANES, 2 * M_QK), F32)
        c_scr[...] = jnp.zeros(c_scr.shape, F32)
        m_scr[...] = jnp.zeros(m_scr.shape, F32)

    for cc in range(qk_ref.shape[0] // L):
        _mlstm_chunk(slice(cc * L, (cc + 1) * L), qk_ref, v_ref, o_ref, gt_ref, cw_ref, cb_ref, hg_ref,
                     y_ref, ext_scr, c_scr, m_scr)


def _mlstm_chunk(rows, qk_ref, v_ref, o_ref, gt_ref, cw_ref, cb_ref, hg_ref, y_ref,
                 ext_scr, c_scr, m_scr):
    L = M_CHUNK
    cur = qk_ref[rows, :].astype(F32)
    ext_scr[SUBLANES:SUBLANES + L, :] = cur
    acc = cb_ref[...] + cw_ref[CONV_W - 1:CONV_W, :] * cur
    for j in range(CONV_W - 1):
        off = SUBLANES - (CONV_W - 1) + j
        acc = acc + cw_ref[j:j + 1, :] * ext_scr[off:off + L, :]
    ext_scr[0:SUBLANES, :] = cur[L - SUBLANES:L, :]
    qk = acc * _sigmoid(acc)

    gt = gt_ref[rows, :]
    lf = jnp.minimum(gt, 0.0) - jnp.log(1.0 + jnp.exp(-jnp.abs(gt)))
    tri = _tril(L)
    bcols = jnp.dot(tri.astype(F32), lf, precision=HIGHEST, preferred_element_type=F32)
    gt_t = gt.T
    b_t = bcols.T
    ones_ext = jnp.ones((L, LANES), BF16)

    for h in range(M_HEADS):
        icol = gt[:, h:h + 1]
        bcol = bcols[:, M_HEADS + h:M_HEADS + h + 1]
        irow = gt_t[h:h + 1, :]
        brow = b_t[M_HEADS + h:M_HEADS + h + 1, :]
        m_prev = m_scr[h, 0:1, 0:1]
        b_last = bcol[L - 1:L, :]

        dmat = jnp.where(tri, bcol - brow + irow, -jnp.inf)
        inter = bcol + m_prev
        m_t = jnp.maximum(inter, jnp.max(dmat, axis=-1, keepdims=True))
        dexp = jnp.exp(dmat - m_t)
        inter_w = jnp.exp(inter - m_t)

        q = (qk[:, h * M_QK_DIM:(h + 1) * M_QK_DIM] * (M_QK_DIM ** -0.5)).astype(BF16)
        kf = qk[:, M_QK + h * M_QK_DIM:M_QK + (h + 1) * M_QK_DIM]
        v_ext = jnp.concatenate([v_ref[rows, h * M_V_DIM:(h + 1) * M_V_DIM], ones_ext], axis=1)

        scores = (_dot_nt(q, kf.astype(BF16)) * dexp).astype(BF16)
        c_old = c_scr[h]
        nd = _dot(scores, v_ext) + inter_w * _dot(q, c_old.astype(BF16))
        den = nd[:, M_V_DIM:]
        inv = 1.0 / jnp.maximum(jnp.abs(den), jnp.exp(-m_t))
        hs = [nd[:, j * LANES:(j + 1) * LANES] * inv for j in range(M_V_DIM // LANES)]
        hh = jnp.concatenate(hs, axis=1)

        m_new = m_t[L - 1:L, :]
        w_s = jnp.exp(b_last - bcol + icol - m_new)
        decay = jnp.exp(b_last + m_prev - m_new)
        kw = (kf * w_s).astype(BF16)
        c_scr[h] = decay * c_old + _dot_tn(kw, v_ext)
        m_scr[h] = jnp.broadcast_to(m_new, (SUBLANES, LANES))

        hn = hh * lax.rsqrt(jnp.mean(hh * hh, axis=-1, keepdims=True) + EPS)
        sl = slice(h * M_V_DIM, (h + 1) * M_V_DIM)
        og = _sigmoid(o_ref[rows, sl].astype(F32))
        y_ref[rows, sl] = (og * hn * hg_ref[:, sl]).astype(BF16)


def _mlstm(proj, gates, conv_w, conv_b, head_g, nb, nc):
    t = proj.shape[0]
    L = M_CHUNK * M_CHUNKS_PER_STEP
    nc = nc // M_CHUNKS_PER_STEP
    row = lambda b, c: b * nc + c
    return pl.pallas_call(
        _mlstm_kernel,
        out_shape=jax.ShapeDtypeStruct((t, M_V), BF16),
        grid=(nb, nc),
        in_specs=[
            pl.BlockSpec((L, 2 * M_QK), lambda b, c: (row(b, c), 0)),
            pl.BlockSpec((L, M_V), lambda b, c: (row(b, c), 1)),
            pl.BlockSpec((L, M_V), lambda b, c: (row(b, c), 2)),
            pl.BlockSpec((L, LANES), lambda b, c: (row(b, c), 0)),
            pl.BlockSpec((CONV_W, 2 * M_QK), lambda b, c: (0, 0)),
            pl.BlockSpec((1, 2 * M_QK), lambda b, c: (0, 0)),
            pl.BlockSpec((1, M_V), lambda b, c: (0, 0)),
        ],
        out_specs=pl.BlockSpec((L, M_V), lambda b, c: (row(b, c), 0)),
        scratch_shapes=[
            pltpu.VMEM((SUBLANES + M_CHUNK, 2 * M_QK), F32),
            pltpu.VMEM((M_HEADS, M_QK_DIM, M_V_DIM + LANES), F32),
            pltpu.VMEM((M_HEADS, SUBLANES, LANES), F32),
        ],
        compiler_params=_params("parallel", "arbitrary"),
        name="mlstm",
    )(proj, proj, proj, gates, conv_w, conv_b, head_g)


def _hgrn_kernel(gf_ref, gi_ref, gq_ref, gg_ref, lb_ref, hg_ref, y_ref, st_scr):
    L, SUB, NSUB = G_CHUNK, G_SUB, G_CHUNK // G_SUB

    @pl.when(pl.program_id(1) == 0)
    def _():
        st_scr[...] = jnp.zeros(st_scr.shape, F32)

    lb = lb_ref[...]
    for cc in range(gf_ref.shape[0] // L):
        _hgrn_chunk(slice(cc * L, (cc + 1) * L), lb, gf_ref, gi_ref, gq_ref, gg_ref, hg_ref, y_ref, st_scr)


def _hgrn_chunk(rows, lb, gf_ref, gi_ref, gq_ref, gg_ref, hg_ref, y_ref, st_scr):
    L, SUB, NSUB = G_CHUNK, G_SUB, G_CHUNK // G_SUB
    f = lb + (1.0 - lb) * _sigmoid(gf_ref[rows, :].astype(F32))
    k_all = 1.0 - f
    cum_all = jnp.dot(_tril(L).astype(F32), jnp.log(f), precision=HIGHEST,
                      preferred_element_type=F32)
    gq = gq_ref[rows, :].astype(F32)
    q_all = gq * _sigmoid(gq)

    row = lax.broadcasted_iota(jnp.int32, (L, G_DIM), 0)
    row_sub = lax.broadcasted_iota(jnp.int32, (SUB, G_DIM), 0)
    ones_sq = jnp.ones((G_DIM, G_DIM), BF16)

    for h in range(G_HEADS):
        sl = slice(h * G_DIM, (h + 1) * G_DIM)
        cum, q, k = cum_all[:, sl], q_all[:, sl], k_all[:, sl]
        iv = gi_ref[rows, sl]
        ivf = iv.astype(F32)
        ends = [cum[SUB * b + SUB - 1:SUB * b + SUB, :] for b in range(NSUB)]
        last = ends[NSUB - 1]
        end_of_row = jnp.concatenate([jnp.broadcast_to(e, (SUB, G_DIM)) for e in ends], axis=0)

        khat = k * jnp.exp(end_of_row - cum)
        qs, ks = [], []
        for b in range(NSUB - 1):
            lo = SUB * (b + 1)
            qs.append(jnp.where(row >= lo, q * jnp.exp(jnp.minimum(cum - ends[b], 0.0)), 0.0))
            ks.append(jnp.where((row >= lo - SUB) & (row < lo), khat, 0.0))
        att = _dot_nt(jnp.concatenate(qs, axis=1).astype(BF16),
                      jnp.concatenate(ks, axis=1).astype(BF16))
        o = _dot(att.astype(BF16), iv)

        st = st_scr[h]
        o = o + _dot_nt((q * jnp.exp(cum)).astype(BF16), st.astype(BF16))

        o_diag = []
        H = SUBLANES
        for a in range(NSUB):
            r0 = SUB * a
            q_a, k_a, c_a, i_a = (z[r0:r0 + SUB, :] for z in (q, k, cum, ivf))
            ms = []
            for s in range(SUB):
                lo = 0 if s < H else H
                diff = c_a[lo:, :] - c_a[s:s + 1, :]
                part = jnp.where(row_sub[lo:lo + H, :] >= s, diff[0:H, :], -jnp.inf)
                diff = jnp.concatenate([part, diff[H:, :]], axis=0) if lo == 0 else part
                ms.append(q_a[lo:, :] * (k_a[s:s + 1, :] * jnp.exp(diff)))
            red = _dot(jnp.concatenate(ms, axis=0).astype(BF16), ones_sq)
            acc_lo = red[0:H, :] * i_a[0:1, :]
            acc_hi = red[H:SUB, :] * i_a[0:1, :]
            for s in range(1, H):
                acc_lo = acc_lo + red[s * SUB:s * SUB + H, :] * i_a[s:s + 1, :]
                acc_hi = acc_hi + red[s * SUB + H:(s + 1) * SUB, :] * i_a[s:s + 1, :]
            for s in range(H, SUB):
                off = H * SUB + (s - H) * H
                acc_hi = acc_hi + red[off:off + H, :] * i_a[s:s + 1, :]
            o_diag += [acc_lo, acc_hi]
        o = o + jnp.concatenate(o_diag, axis=0)

        kd = (k * jnp.exp(last - cum)).astype(BF16)
        st_scr[h] = jnp.exp(last) * st + _dot_tn(iv, kd)

        hn = o * lax.rsqrt(jnp.mean(o * o, axis=-1, keepdims=True) + EPS)
        gg = gg_ref[rows, sl].astype(F32)
        y_ref[rows, sl] = (hn * hg_ref[:, sl] * (gg * _sigmoid(gg))).astype(BF16)


def _hgrn(proj, lb, head_g, nb, nc, col0):
    t = proj.shape[0]
    L = G_CHUNK * G_CHUNKS_PER_STEP
    nc = nc // G_CHUNKS_PER_STEP
    row = lambda b, c: b * nc + c
    spec = lambda j: pl.BlockSpec((L, G_W), lambda b, c: (row(b, c), col0 + j))
    return pl.pallas_call(
        _hgrn_kernel,
        out_shape=jax.ShapeDtypeStruct((t, G_W), BF16),
        grid=(nb, nc),
        in_specs=[spec(0), spec(1), spec(2), spec(3),
                  pl.BlockSpec((1, G_W), lambda b, c: (0, 0)),
                  pl.BlockSpec((1, G_W), lambda b, c: (0, 0))],
        out_specs=pl.BlockSpec((L, G_W), lambda b, c: (row(b, c), 0)),
        scratch_shapes=[pltpu.VMEM((G_HEADS, G_DIM, G_DIM), F32)],
        compiler_params=_params("parallel", "arbitrary"),
        name="hgrn2",
    )(proj, proj, proj, proj, lb, head_g)


def _merge_kernel(x_ref, ym_ref, yg_ref, am_ref, ag_ref, wm_ref, wg_ref, wo_ref, g2_ref,
                  x1_ref, u2_ref):
    zm = _dot(ym_ref[...], wm_ref[...])
    zg = _dot(yg_ref[...], wg_ref[...])
    z = _sigmoid(am_ref[...].astype(F32)) * zm + _sigmoid(ag_ref[...].astype(F32)) * zg
    x1 = x_ref[...] + _dot(z.astype(BF16), wo_ref[...])
    x1_ref[...] = x1
    u2 = x1 * lax.rsqrt(jnp.mean(x1 * x1, axis=-1, keepdims=True) + EPS) * g2_ref[...]
    u2_ref[...] = u2.astype(BF16)


def _merge(x2, ym, yg, proj, wm, wg, wo, g2, tm, col_am):
    t, d = x2.shape
    rows = lambda w: pl.BlockSpec((tm, w), lambda i: (i, 0))
    full = lambda a: pl.BlockSpec(a.shape, lambda i: (0, 0))
    return pl.pallas_call(
        _merge_kernel,
        out_shape=(jax.ShapeDtypeStruct((t, d), F32), jax.ShapeDtypeStruct((t, d), BF16)),
        grid=(t // tm,),
        in_specs=[rows(d), rows(M_V), rows(G_W),
                  pl.BlockSpec((tm, d), lambda i: (i, col_am)),
                  pl.BlockSpec((tm, d), lambda i: (i, col_am + 1)),
                  full(wm), full(wg), full(wo), full(g2)],
        out_specs=(rows(d), rows(d)),
        compiler_params=_params("parallel"),
        name="merge",
    )(x2, ym, yg, proj, proj, wm, wg, wo, g2)


def _sort16_pairs():
    def merge(lo, hi, r):
        step = r * 2
        if step < hi - lo:
            yield from merge(lo, hi, step)
            yield from merge(lo + r, hi, step)
            yield from [(i, i + r) for i in range(lo + r, hi - r, step)]
        else:
            yield (lo, lo + r)

    def sort(lo, hi):
        if hi - lo >= 1:
            mid = lo + (hi - lo) // 2
            yield from sort(lo, mid)
            yield from sort(mid + 1, hi)
            yield from merge(lo, hi, 1)

    return list(sort(0, P_TOPK - 1))


def _top_rows(vals, k):
    out = []
    for _ in range(k):
        m = jnp.max(vals, axis=0, keepdims=True)
        out.append(m)
        vals = jnp.where(vals == m, -jnp.inf, vals)
    return out


def _top_of_128(vals, k):
    n = P_TOPK
    v = [vals[SUBLANES * g:SUBLANES * (g + 1), :] for g in range(n)]
    for a, b in _sort16_pairs():
        v[a], v[b] = jnp.maximum(v[a], v[b]), jnp.minimum(v[a], v[b])
    out = []
    for r in range(k):
        m = jnp.max(v[0], axis=0, keepdims=True)
        out.append(m)
        if r == k - 1:
            break
        hit = v[0] == m
        depth = k - 1 - r
        for i in range(min(depth, n - 1)):
            v[i] = jnp.where(hit, v[i + 1], v[i])
        if depth >= n:
            v[n - 1] = jnp.where(hit, -jnp.inf, v[n - 1])
    return out


def _route_kernel(u2_ref, wq_ref, keys_ref, thr_ref, w0_ref, s1_ref, e1_ref, cand_scr):
    K = P_TOPK
    tb = u2_ref.shape[0]
    q_t = _dot_nt(wq_ref[...], u2_ref[...])
    pairs = [(r, c) for r in range(K + 1) for c in range(K + 1) if (r + 1) * (c + 1) <= K + 1]
    n_pad = cand_scr.shape[0] - len(pairs)
    cand_scr[len(pairs):, :] = jnp.full((n_pad, tb), -jnp.inf, F32)

    for h in range(P_HEADS):
        s = []
        for p in range(2):
            r0 = (2 * h + p) * P_NKEYS
            s.append(_dot(keys_ref[2 * h + p], q_t[r0:r0 + P_NKEYS, :].astype(BF16)))
        a = _top_of_128(s[0], K + 1)
        b = _top_of_128(s[1], K + 1)
        for n, (r, c) in enumerate(pairs):
            cand_scr[n:n + 1, :] = a[r] + b[c]
        top = _top_rows(cand_scr[...], K + 1)
        z = jnp.ones_like(top[0])
        for r in range(1, K):
            z = z + jnp.exp(top[r] - top[0])
        tau = 0.5 * (top[K - 1] + top[K])
        thr_ref[h] = tau - s[0]
        w0_ref[h] = 0.5 * jnp.exp(s[0] - a[0]) / z
        s1_ref[h] = s[1]
        e1_ref[h] = jnp.exp(s[1] - b[0])


def _route(u2, wq_t, keys, tb):
    t, d = u2.shape
    n_pairs = sum(1 for r in range(P_TOPK + 1) for c in range(P_TOPK + 1)
                  if (r + 1) * (c + 1) <= P_TOPK + 1)
    n_cand = -(-n_pairs // SUBLANES) * SUBLANES
    out = jax.ShapeDtypeStruct((P_HEADS, P_NKEYS, t), F32)
    ospec = pl.BlockSpec((P_HEADS, P_NKEYS, tb), lambda i: (0, 0, i))
    return pl.pallas_call(
        _route_kernel,
        out_shape=(out, out, out, out),
        grid=(t // tb,),
        in_specs=[pl.BlockSpec((tb, d), lambda i: (i, 0)),
                  pl.BlockSpec(wq_t.shape, lambda i: (0, 0)),
                  pl.BlockSpec(keys.shape, lambda i: (0, 0, 0))],
        out_specs=(ospec, ospec, ospec, ospec),
        scratch_shapes=[pltpu.VMEM((n_cand, tb), F32)],
        compiler_params=_params("parallel"),
        name="route",
    )(u2, wq_t, keys)


PEER_JB = 32
PEER_IB = 4
PEER_TILES_PER_REGION = 2
MXU_TILE = 256
N_MXU = 2
MXU_ROWS = 64
ACC_ROWS_PER_ENTRY = 4


def _gate_tile(thr_ref, w0_ref, s1_ref, e1_ref, h_in, p_out, ib, tc, jb):
    lanes = slice(tc * LANES, (tc + 1) * LANES)
    jrows = slice(jb * PEER_JB, (jb + 1) * PEER_JB)
    gate = [None] * PEER_IB
    for h in range(P_HEADS):
        s1 = s1_ref[h, jrows, lanes]
        e1 = e1_ref[h, jrows, lanes]
        for k in range(PEER_IB):
            i = ib * PEER_IB + k
            term = (jnp.where(s1 >= thr_ref[h, i:i + 1, lanes], e1, 0.0)
                    * w0_ref[h, i:i + 1, lanes])
            gate[k] = term if gate[k] is None else gate[k] + term
    for k in range(PEER_IB):
        r0 = (ib * PEER_IB + k) * P_NKEYS + jb * PEER_JB
        hv = h_in[r0:r0 + PEER_JB, lanes]
        act = hv * (1.0 + lax.erf(hv * (2.0 ** -0.5)))
        p_out[r0:r0 + PEER_JB, lanes] = (act * gate[k]).astype(BF16)


def _peer_step(u2_ref, u_ref, vt_ref, thr_ref, w0_ref, s1_ref, e1_ref, acc_scr,
               h_in, h_out, p_in, p_out):
    f = pl.program_id(0)
    eb, d = u_ref.shape
    tb = u2_ref.shape[0]
    assert tb == N_MXU * MXU_TILE and eb == d
    n_kt = d // MXU_TILE
    n_c = eb // MXU_ROWS
    per_phase = n_kt * n_c
    tiles = [(ib, tc, jb) for ib in range(eb // (PEER_IB * P_NKEYS))
             for tc in range(tb // LANES) for jb in range(P_NKEYS // PEER_JB)]
    assert len(tiles) % PEER_TILES_PER_REGION == 0
    n_reg = len(tiles) // PEER_TILES_PER_REGION
    assert (2 * per_phase) % n_reg == 0
    per_reg = 2 * per_phase // n_reg

    def weights(w):
        phase, kt = divmod(w, n_kt)
        ks = slice(kt * MXU_TILE, (kt + 1) * MXU_TILE)
        for q in range(N_MXU):
            ts = slice(q * MXU_TILE, (q + 1) * MXU_TILE)
            if phase == 0:
                pltpu.matmul_push_rhs(u2_ref[ts, ks], staging_register=w % 2, mxu_index=q,
                                      transpose=True)
            else:
                pltpu.matmul_push_rhs(p_in[ks, ts], staging_register=w % 2, mxu_index=q)

    def piece(idx):
        w, c = divmod(idx, n_c)
        phase, kt = divmod(w, n_kt)
        rows = slice(c * MXU_ROWS, (c + 1) * MXU_ROWS)
        ks = slice(kt * MXU_TILE, (kt + 1) * MXU_TILE)
        addr = c * (MXU_ROWS // ACC_ROWS_PER_ENTRY)
        if c == n_c // 2 and w + 1 < 2 * n_kt:
            weights(w + 1)
        for q in range(N_MXU):
            ts = slice(q * MXU_TILE, (q + 1) * MXU_TILE)
            if kt == 0 and phase == 0:
                acc_scr[rows, ts] += pltpu.matmul_pop(addr, (MXU_ROWS, MXU_TILE), F32, mxu_index=q)
            if kt == 0 and phase == 1:
                h_out[rows, ts] = pltpu.matmul_pop(addr, (MXU_ROWS, MXU_TILE), F32, mxu_index=q)
            lhs = u_ref[rows, ks] if phase == 0 else vt_ref[rows, ks]
            pltpu.matmul_acc_lhs(addr, lhs, mxu_index=q,
                                 load_staged_rhs=(w % 2) if c == 0 else None)

    weights(0)
    for r in range(n_reg):
        @pl.when(f >= -r)
        def _(r=r):
            for idx in range(r * per_reg, (r + 1) * per_reg):
                piece(idx)
            for tile in tiles[r * PEER_TILES_PER_REGION:(r + 1) * PEER_TILES_PER_REGION]:
                _gate_tile(thr_ref, w0_ref, s1_ref, e1_ref, h_in, p_out, *tile)


def _peer_kernel(n_e, n_steps, u2_ref, x1_ref, fg_ref, u_ref, vt_ref, thr_ref, w0_ref, s1_ref,
                 e1_ref, out_ref, acc_scr, h_scr, p_scr):
    f = pl.program_id(0)
    full = (h_scr.shape[1], MXU_TILE)

    @pl.when(f == 0)
    def _():
        acc_scr[...] = jnp.zeros(acc_scr.shape, F32)
        p_scr[1] = jnp.zeros(p_scr.shape[1:], BF16)
        for q in range(N_MXU):
            h_scr[0, :, q * MXU_TILE:(q + 1) * MXU_TILE] = pltpu.matmul_pop(0, full, F32, mxu_index=q)
        h_scr[1] = jnp.zeros(h_scr.shape[1:], F32)

    step = functools.partial(_peer_step, u2_ref, u_ref, vt_ref, thr_ref, w0_ref, s1_ref, e1_ref,
                             acc_scr)

    @pl.when(f % 2 == 0)
    def _():
        step(h_scr.at[1], h_scr.at[0], p_scr.at[1], p_scr.at[0])

    @pl.when(f % 2 == 1)
    def _():
        step(h_scr.at[0], h_scr.at[1], p_scr.at[0], p_scr.at[1])

    @pl.when((f >= 3) & ((f - 3) % n_e == n_e - 1))
    def _():
        x2 = x1_ref[...] + acc_scr[...].T
        out_ref[...] = (x2 * lax.rsqrt(jnp.mean(x2 * x2, axis=-1, keepdims=True) + EPS)
                        * fg_ref[...])
        acc_scr[...] = jnp.zeros(acc_scr.shape, F32)

    @pl.when(f == n_steps - 1)
    def _():
        for q in range(N_MXU):
            h_scr[0, :, q * MXU_TILE:(q + 1) * MXU_TILE] = pltpu.matmul_pop(0, full, F32, mxu_index=q)


def _peer(u2, x1, fg, u_emb, v_t, thr, w0, s1, e1, tb, eb):
    t, d = u2.shape
    n_e = u_emb.shape[0] // eb
    n_i = eb // P_NKEYS
    n = (t // tb) * n_e
    n_steps = n + 3
    blk = lambda f, lag: jnp.clip(f - lag, 0, n - 1)
    tok = lambda lag: (lambda f: (blk(f, lag) // n_e, 0))
    gate_blk = lambda f: (0, blk(f, 1) % n_e, blk(f, 1) // n_e)
    key_blk = lambda f: (0, 0, blk(f, 1) // n_e)
    return pl.pallas_call(
        functools.partial(_peer_kernel, n_e, n_steps),
        out_shape=jax.ShapeDtypeStruct((t, d), F32),
        grid=(n_steps,),
        in_specs=[pl.BlockSpec((tb, d), tok(0)),
                  pl.BlockSpec((tb, d), tok(3)),
                  pl.BlockSpec((1, d), lambda f: (0, 0)),
                  pl.BlockSpec((eb, d), lambda f: (blk(f, 0) % n_e, 0)),
                  pl.BlockSpec((d, eb), lambda f: (0, blk(f, 2) % n_e)),
                  pl.BlockSpec((P_HEADS, n_i, tb), gate_blk),
                  pl.BlockSpec((P_HEADS, n_i, tb), gate_blk),
                  pl.BlockSpec((P_HEADS, P_NKEYS, tb), key_blk),
                  pl.BlockSpec((P_HEADS, P_NKEYS, tb), key_blk)],
        out_specs=pl.BlockSpec((tb, d), tok(3)),
        scratch_shapes=[pltpu.VMEM((d, tb), F32), pltpu.VMEM((2, eb, tb), F32),
                        pltpu.VMEM((2, eb, tb), BF16)],
        compiler_params=_params("arbitrary"),
        name="peer",
    )(u2, x1, fg, u_emb, v_t, thr, w0, s1, e1)


def _tiles(t):
    pick = lambda pref: next(c for c in (pref, 1024, 512, 256, 128) if c <= pref and t % c == 0)
    return dict(inproj_tm=pick(2048), inproj_tn=1024, merge_tm=pick(512),
                route_tb=pick(256), peer_tb=pick(512), peer_eb=1024)


def kernel(x, norm1_g, w_in, b_in, conv_w, conv_b, m_head_g, lb_table, g_head_g, w_br_m, w_br_g,
           w_out, norm2_g, w_pq, sub_keys, u_emb, v_emb, final_g):
    nb, seq, d = x.shape
    t = nb * seq
    depth = w_in.shape[0]
    assert d == 1024 and seq % M_CHUNK == 0 and t % LANES == 0
    tl = _tiles(t)
    n_gate = 2 * M_HEADS
    g0 = 2 * M_QK + 2 * M_V
    assert (w_in.shape[2] - n_gate) % tl["inproj_tn"] == 0

    lb_all = jnp.cumsum(jax.nn.softmax(lb_table.astype(F32), axis=0), axis=0)
    x2 = x.reshape(t, d)
    row = lambda a: a.reshape(1, -1).astype(F32)

    for l in range(depth):
        w_main = jnp.concatenate([w_in[l][:, :g0], w_in[l][:, g0 + n_gate:]], axis=1).astype(BF16)
        b_main = jnp.concatenate([b_in[l][:g0], b_in[l][g0 + n_gate:]]).reshape(1, -1)
        w_gate = jnp.pad(w_in[l][:, g0:g0 + n_gate], ((0, 0), (0, LANES - n_gate))).astype(BF16)
        b_gate = jnp.pad(b_in[l][g0:g0 + n_gate], (0, LANES - n_gate)).reshape(1, -1)

        proj, gates = _inproj(x2, row(norm1_g[l]), w_main, b_main, w_gate, b_gate,
                              tl["inproj_tm"], tl["inproj_tn"])
        y_m = _mlstm(proj, gates, conv_w[l], row(conv_b[l]), row(m_head_g[l]),
                     nb, seq // M_CHUNK)
        y_g = _hgrn(proj, row(lb_all[l]), row(g_head_g[l]), nb, seq // G_CHUNK, g0 // G_W)
        x1, u2 = _merge(x2, y_m, y_g, proj, w_br_m[l].astype(BF16), w_br_g[l].astype(BF16),
                        w_out[l].astype(BF16), row(norm2_g[l]), tl["merge_tm"],
                        (g0 + 4 * G_W) // d)

        keys = sub_keys[l].reshape(2 * P_HEADS, P_NKEYS, -1).astype(BF16)
        thr, w0, s1, e1 = _route(u2, w_pq[l].T.astype(BF16), keys, tl["route_tb"])
        assert depth == 1
        x2 = _peer(u2, x1, row(final_g), u_emb[l].astype(BF16), v_emb[l].T.astype(BF16),
                   thr, w0, s1, e1, tl["peer_tb"], tl["peer_eb"])
    return x2.reshape(nb, seq, d)
```

```python
import functools

import jax
import jax.numpy as jnp
from jax import lax
from jax.experimental import pallas as pl
from jax.experimental.pallas import tpu as pltpu

F32 = jnp.float32
BF16 = jnp.bfloat16
HIGHEST = lax.Precision.HIGHEST

EPS = 1e-6
LOG2E = 1.4426950408889634
LANES = 128
SUBLANES = 8
VMEM_LIMIT = 48 * 1024 * 1024

M_HEADS, M_QK_DIM, M_V_DIM, M_CHUNK, CONV_W = 4, 256, 512, 128, 4
M_CHUNKS_PER_STEP = 4
G_HEADS, G_DIM, G_CHUNK, G_SUB = 8, 128, 64, 16
G_CHUNKS_PER_STEP = 8
P_HEADS, P_NKEYS, P_TOPK = 8, 128, 16
M_QK = M_HEADS * M_QK_DIM
M_V = M_HEADS * M_V_DIM
G_W = G_HEADS * G_DIM


def _params(*sem):
    return pltpu.CompilerParams(dimension_semantics=sem, vmem_limit_bytes=VMEM_LIMIT)


def _dot(a, b):
    return jnp.dot(a, b, preferred_element_type=F32)


def _dot_nt(a, b):
    return lax.dot_general(a, b, (((1,), (1,)), ((), ())), preferred_element_type=F32)


def _dot_tn(a, b):
    return lax.dot_general(a, b, (((0,), (0,)), ((), ())), preferred_element_type=F32)


def _sigmoid(x):
    return 1.0 / (1.0 + jnp.exp2(x * (-LOG2E)))


def _tril(n):
    r = lax.broadcasted_iota(jnp.int32, (n, n), 0)
    c = lax.broadcasted_iota(jnp.int32, (n, n), 1)
    return (r >= c)


def _inproj_kernel(x_ref, g_ref, w_ref, b_ref, wg_ref, bg_ref, out_ref, gate_ref, u_scr):
    @pl.when(pl.program_id(1) == 0)
    def _():
        x = x_ref[...]
        u = x * lax.rsqrt(jnp.mean(x * x, axis=-1, keepdims=True) + EPS) * g_ref[...]
        u_scr[...] = u.astype(BF16)
        gate_ref[...] = _dot(u.astype(BF16), wg_ref[...]) + bg_ref[...]

    out_ref[...] = (_dot(u_scr[...], w_ref[...]) + b_ref[...]).astype(BF16)


def _inproj(x2, g, w, b, wg, bg, tm, tn):
    t, d = x2.shape
    n = w.shape[1]
    return pl.pallas_call(
        _inproj_kernel,
        out_shape=(jax.ShapeDtypeStruct((t, n), BF16), jax.ShapeDtypeStruct((t, LANES), F32)),
        grid=(t // tm, n // tn),
        in_specs=[
            pl.BlockSpec((tm, d), lambda i, j: (i, 0)),
            pl.BlockSpec((1, d), lambda i, j: (0, 0)),
            pl.BlockSpec((d, tn), lambda i, j: (0, j)),
            pl.BlockSpec((1, tn), lambda i, j: (0, j)),
            pl.BlockSpec((d, LANES), lambda i, j: (0, 0)),
            pl.BlockSpec((1, LANES), lambda i, j: (0, 0)),
        ],
        out_specs=(pl.BlockSpec((tm, tn), lambda i, j: (i, j)),
                   pl.BlockSpec((tm, LANES), lambda i, j: (i, 0))),
        scratch_shapes=[pltpu.VMEM((tm, d), BF16)],
        compiler_params=_params("parallel", "arbitrary"),
        name="inproj",
    )(x2, g, w, b, wg, bg)


def _mlstm_kernel(qk_ref, v_ref, o_ref, gt_ref, cw_ref, cb_ref, hg_ref, y_ref,
                  ext_scr, c_scr, m_scr):
    L = M_CHUNK

    @pl.when(pl.program_id(1) == 0)
    def _():
        ext_scr[0:SUBLANES, :] = jnp.zeros((SUBLANES, 2 * M_QK), F32)
        c_scr[...] = jnp.zeros(c_scr.shape, F32)
        m_scr[...] = jnp.zeros(m_scr.shape, F32)

    for cc in range(qk_ref.shape[0] // L):
        _mlstm_chunk(slice(cc * L, (cc + 1) * L), qk_ref, v_ref, o_ref, gt_ref, cw_ref, cb_ref, hg_ref,
                     y_ref, ext_scr, c_scr, m_scr)


def _mlstm_chunk(rows, qk_ref, v_ref, o_ref, gt_ref, cw_ref, cb_ref, hg_ref, y_ref,
                 ext_scr, c_scr, m_scr):
    L = M_CHUNK
    cur = qk_ref[rows, :].astype(F32)
    ext_scr[SUBLANES:SUBLANES + L, :] = cur
    acc = cb_ref[...] + cw_ref[CONV_W - 1:CONV_W, :] * cur
    for j in range(CONV_W - 1):
        off = SUBLANES - (CONV_W - 1) + j
        acc = acc + cw_ref[j:j + 1, :] * ext_scr[off:off + L, :]
    ext_scr[0:SUBLANES, :] = cur[L - SUBLANES:L, :]
    qk = acc * _sigmoid(acc)

    gt = gt_ref[rows, :]
    lf = jnp.minimum(gt, 0.0) - jnp.log(1.0 + jnp.exp(-jnp.abs(gt)))
    tri = _tril(L)
    bcols = jnp.dot(tri.astype(F32), lf, precision=HIGHEST, preferred_element_type=F32)
    gt_t = gt.T
    b_t = bcols.T
    ones_ext = jnp.ones((L, LANES), BF16)

    for h in range(M_HEADS):
        icol = gt[:, h:h + 1]
        bcol = bcols[:, M_HEADS + h:M_HEADS + h + 1]
        irow = gt_t[h:h + 1, :]
        brow = b_t[M_HEADS + h:M_HEADS + h + 1, :]
        m_prev = m_scr[h, 0:1, 0:1]
        b_last = bcol[L - 1:L, :]

        dmat = jnp.where(tri, bcol - brow + irow, -jnp.inf)
        inter = bcol + m_prev
        m_t = jnp.maximum(inter, jnp.max(dmat, axis=-1, keepdims=True))
        dexp = jnp.exp(dmat - m_t)
        inter_w = jnp.exp(inter - m_t)

        q = (qk[:, h * M_QK_DIM:(h + 1) * M_QK_DIM] * (M_QK_DIM ** -0.5)).astype(BF16)
        kf = qk[:, M_QK + h * M_QK_DIM:M_QK + (h + 1) * M_QK_DIM]
        v_ext = jnp.concatenate([v_ref[rows, h * M_V_DIM:(h + 1) * M_V_DIM], ones_ext], axis=1)

        scores = (_dot_nt(q, kf.astype(BF16)) * dexp).astype(BF16)
        c_old = c_scr[h]
        nd = _dot(scores, v_ext) + inter_w * _dot(q, c_old.astype(BF16))
        den = nd[:, M_V_DIM:]
        inv = 1.0 / jnp.maximum(jnp.abs(den), jnp.exp(-m_t))
        hs = [nd[:, j * LANES:(j + 1) * LANES] * inv for j in range(M_V_DIM // LANES)]
        hh = jnp.concatenate(hs, axis=1)

        m_new = m_t[L - 1:L, :]
        w_s = jnp.exp(b_last - bcol + icol - m_new)
        decay = jnp.exp(b_last + m_prev - m_new)
        kw = (kf * w_s).astype(BF16)
        c_scr[h] = decay * c_old + _dot_tn(kw, v_ext)
        m_scr[h] = jnp.broadcast_to(m_new, (SUBLANES, LANES))

        hn = hh * lax.rsqrt(jnp.mean(hh * hh, axis=-1, keepdims=True) + EPS)
        sl = slice(h * M_V_DIM, (h + 1) * M_V_DIM)
        og = _sigmoid(o_ref[rows, sl].astype(F32))
        y_ref[rows, sl] = (og * hn * hg_ref[:, sl]).astype(BF16)


def _mlstm(proj, gates, conv_w, conv_b, head_g, nb, nc):
    t = proj.shape[0]
    L = M_CHUNK * M_CHUNKS_PER_STEP
    nc = nc // M_CHUNKS_PER_STEP
    row = lambda b, c: b * nc + c
    return pl.pallas_call(
        _mlstm_kernel,
        out_shape=jax.ShapeDtypeStruct((t, M_V), BF16),
        grid=(nb, nc),
        in_specs=[
            pl.BlockSpec((L, 2 * M_QK), lambda b, c: (row(b, c), 0)),
            pl.BlockSpec((L, M_V), lambda b, c: (row(b, c), 1)),
            pl.BlockSpec((L, M_V), lambda b, c: (row(b, c), 2)),
            pl.BlockSpec((L, LANES), lambda b, c: (row(b, c), 0)),
            pl.BlockSpec((CONV_W, 2 * M_QK), lambda b, c: (0, 0)),
            pl.BlockSpec((1, 2 * M_QK), lambda b, c: (0, 0)),
            pl.BlockSpec((1, M_V), lambda b, c: (0, 0)),
        ],
        out_specs=pl.BlockSpec((L, M_V), lambda b, c: (row(b, c), 0)),
        scratch_shapes=[
            pltpu.VMEM((SUBLANES + M_CHUNK, 2 * M_QK), F32),
            pltpu.VMEM((M_HEADS, M_QK_DIM, M_V_DIM + LANES), F32),
            pltpu.VMEM((M_HEADS, SUBLANES, LANES), F32),
        ],
        compiler_params=_params("parallel", "arbitrary"),
        name="mlstm",
    )(proj, proj, proj, gates, conv_w, conv_b, head_g)


def _hgrn_kernel(gf_ref, gi_ref, gq_ref, gg_ref, lb_ref, hg_ref, y_ref, st_scr):
    L, SUB, NSUB = G_CHUNK, G_SUB, G_CHUNK // G_SUB

    @pl.when(pl.program_id(1) == 0)
    def _():
        st_scr[...] = jnp.zeros(st_scr.shape, F32)

    lb = lb_ref[...]
    for cc in range(gf_ref.shape[0] // L):
        _hgrn_chunk(slice(cc * L, (cc + 1) * L), lb, gf_ref, gi_ref, gq_ref, gg_ref, hg_ref, y_ref, st_scr)


def _hgrn_chunk(rows, lb, gf_ref, gi_ref, gq_ref, gg_ref, hg_ref, y_ref, st_scr):
    L, SUB, NSUB = G_CHUNK, G_SUB, G_CHUNK // G_SUB
    f = lb + (1.0 - lb) * _sigmoid(gf_ref[rows, :].astype(F32))
    k_all = 1.0 - f
    cum_all = jnp.dot(_tril(L).astype(F32), jnp.log(f), precision=HIGHEST,
                      preferred_element_type=F32)
    gq = gq_ref[rows, :].astype(F32)
    q_all = gq * _sigmoid(gq)

    row = lax.broadcasted_iota(jnp.int32, (L, G_DIM), 0)
    row_sub = lax.broadcasted_iota(jnp.int32, (SUB, G_DIM), 0)
    ones_sq = jnp.ones((G_DIM, G_DIM), BF16)

    for h in range(G_HEADS):
        sl = slice(h * G_DIM, (h + 1) * G_DIM)
        cum, q, k = cum_all[:, sl], q_all[:, sl], k_all[:, sl]
        iv = gi_ref[rows, sl]
        ivf = iv.astype(F32)
        ends = [cum[SUB * b + SUB - 1:SUB * b + SUB, :] for b in range(NSUB)]
        last = ends[NSUB - 1]
        end_of_row = jnp.concatenate([jnp.broadcast_to(e, (SUB, G_DIM)) for e in ends], axis=0)

        khat = k * jnp.exp(end_of_row - cum)
        qs, ks = [], []
        for b in range(NSUB - 1):
            lo = SUB * (b + 1)
            qs.append(jnp.where(row >= lo, q * jnp.exp(jnp.minimum(cum - ends[b], 0.0)), 0.0))
            ks.append(jnp.where((row >= lo - SUB) & (row < lo), khat, 0.0))
        att = _dot_nt(jnp.concatenate(qs, axis=1).astype(BF16),
                      jnp.concatenate(ks, axis=1).astype(BF16))
        o = _dot(att.astype(BF16), iv)

        st = st_scr[h]
        o = o + _dot_nt((q * jnp.exp(cum)).astype(BF16), st.astype(BF16))

        o_diag = []
        H = SUBLANES
        for a in range(NSUB):
            r0 = SUB * a
            q_a, k_a, c_a, i_a = (z[r0:r0 + SUB, :] for z in (q, k, cum, ivf))
            ms = []
            for s in range(SUB):
                lo = 0 if s < H else H
                diff = c_a[lo:, :] - c_a[s:s + 1, :]
                part = jnp.where(row_sub[lo:lo + H, :] >= s, diff[0:H, :], -jnp.inf)
                diff = jnp.concatenate([part, diff[H:, :]], axis=0) if lo == 0 else part
                ms.append(q_a[lo:, :] * (k_a[s:s + 1, :] * jnp.exp(diff)))
            red = _dot(jnp.concatenate(ms, axis=0).astype(BF16), ones_sq)
            acc_lo = red[0:H, :] * i_a[0:1, :]
            acc_hi = red[H:SUB, :] * i_a[0:1, :]
            for s in range(1, H):
                acc_lo = acc_lo + red[s * SUB:s * SUB + H, :] * i_a[s:s + 1, :]
                acc_hi = acc_hi + red[s * SUB + H:(s + 1) * SUB, :] * i_a[s:s + 1, :]
            for s in range(H, SUB):
                off = H * SUB + (s - H) * H
                acc_hi = acc_hi + red[off:off + H, :] * i_a[s:s + 1, :]
            o_diag += [acc_lo, acc_hi]
        o = o + jnp.concatenate(o_diag, axis=0)

        kd = (k * jnp.exp(last - cum)).astype(BF16)
        st_scr[h] = jnp.exp(last) * st + _dot_tn(iv, kd)

        hn = o * lax.rsqrt(jnp.mean(o * o, axis=-1, keepdims=True) + EPS)
        gg = gg_ref[rows, sl].astype(F32)
        y_ref[rows, sl] = (hn * hg_ref[:, sl] * (gg * _sigmoid(gg))).astype(BF16)


def _hgrn(proj, lb, head_g, nb, nc, col0):
    t = proj.shape[0]
    L = G_CHUNK * G_CHUNKS_PER_STEP
    nc = nc // G_CHUNKS_PER_STEP
    row = lambda b, c: b * nc + c
    spec = lambda j: pl.BlockSpec((L, G_W), lambda b, c: (row(b, c), col0 + j))
    return pl.pallas_call(
        _hgrn_kernel,
        out_shape=jax.ShapeDtypeStruct((t, G_W), BF16),
        grid=(nb, nc),
        in_specs=[spec(0), spec(1), spec(2), spec(3),
                  pl.BlockSpec((1, G_W), lambda b, c: (0, 0)),
                  pl.BlockSpec((1, G_W), lambda b, c: (0, 0))],
        out_specs=pl.BlockSpec((L, G_W), lambda b, c: (row(b, c), 0)),
        scratch_shapes=[pltpu.VMEM((G_HEADS, G_DIM, G_DIM), F32)],
        compiler_params=_params("parallel", "arbitrary"),
        name="hgrn2",
    )(proj, proj, proj, proj, lb, head_g)


def _merge_kernel(x_ref, ym_ref, yg_ref, am_ref, ag_ref, wm_ref, wg_ref, wo_ref, g2_ref,
                  x1_ref, u2_ref):
    zm = _dot(ym_ref[...], wm_ref[...])
    zg = _dot(yg_ref[...], wg_ref[...])
    z = _sigmoid(am_ref[...].astype(F32)) * zm + _sigmoid(ag_ref[...].astype(F32)) * zg
    x1 = x_ref[...] + _dot(z.astype(BF16), wo_ref[...])
    x1_ref[...] = x1
    u2 = x1 * lax.rsqrt(jnp.mean(x1 * x1, axis=-1, keepdims=True) + EPS) * g2_ref[...]
    u2_ref[...] = u2.astype(BF16)


def _merge(x2, ym, yg, proj, wm, wg, wo, g2, tm, col_am):
    t, d = x2.shape
    rows = lambda w: pl.BlockSpec((tm, w), lambda i: (i, 0))
    full = lambda a: pl.BlockSpec(a.shape, lambda i: (0, 0))
    return pl.pallas_call(
        _merge_kernel,
        out_shape=(jax.ShapeDtypeStruct((t, d), F32), jax.ShapeDtypeStruct((t, d), BF16)),
        grid=(t // tm,),
        in_specs=[rows(d), rows(M_V), rows(G_W),
                  pl.BlockSpec((tm, d), lambda i: (i, col_am)),
                  pl.BlockSpec((tm, d), lambda i: (i, col_am + 1)),
                  full(wm), full(wg), full(wo), full(g2)],
        out_specs=(rows(d), rows(d)),
        compiler_params=_params("parallel"),
        name="merge",
    )(x2, ym, yg, proj, proj, wm, wg, wo, g2)


def _sort16_pairs():
    def merge(lo, hi, r):
        step = r * 2
        if step < hi - lo:
            yield from merge(lo, hi, step)
            yield from merge(lo + r, hi, step)
            yield from [(i, i + r) for i in range(lo + r, hi - r, step)]
        else:
            yield (lo, lo + r)

    def sort(lo, hi):
        if hi - lo >= 1:
            mid = lo + (hi - lo) // 2
            yield from sort(lo, mid)
            yield from sort(mid + 1, hi)
            yield from merge(lo, hi, 1)

    return list(sort(0, P_TOPK - 1))


def _top_rows(vals, k):
    out = []
    for _ in range(k):
        m = jnp.max(vals, axis=0, keepdims=True)
        out.append(m)
        vals = jnp.where(vals == m, -jnp.inf, vals)
    return out


def _top_of_128(vals, k):
    n = P_TOPK
    v = [vals[SUBLANES * g:SUBLANES * (g + 1), :] for g in range(n)]
    for a, b in _sort16_pairs():
        v[a], v[b] = jnp.maximum(v[a], v[b]), jnp.minimum(v[a], v[b])
    out = []
    for r in range(k):
        m = jnp.max(v[0], axis=0, keepdims=True)
        out.append(m)
        if r == k - 1:
            break
        hit = v[0] == m
        depth = k - 1 - r
        for i in range(min(depth, n - 1)):
            v[i] = jnp.where(hit, v[i + 1], v[i])
        if depth >= n:
            v[n - 1] = jnp.where(hit, -jnp.inf, v[n - 1])
    return out


def _route_kernel(u2_ref, wq_ref, keys_ref, thr_ref, w0_ref, s1_ref, e1_ref, cand_scr):
    K = P_TOPK
    tb = u2_ref.shape[0]
    q_t = _dot_nt(wq_ref[...], u2_ref[...])
    pairs = [(r, c) for r in range(K + 1) for c in range(K + 1) if (r + 1) * (c + 1) <= K + 1]
    n_pad = cand_scr.shape[0] - len(pairs)
    cand_scr[len(pairs):, :] = jnp.full((n_pad, tb), -jnp.inf, F32)

    for h in range(P_HEADS):
        s = []
        for p in range(2):
            r0 = (2 * h + p) * P_NKEYS
            s.append(_dot(keys_ref[2 * h + p], q_t[r0:r0 + P_NKEYS, :].astype(BF16)))
        a = _top_of_128(s[0], K + 1)
        b = _top_of_128(s[1], K + 1)
        for n, (r, c) in enumerate(pairs):
            cand_scr[n:n + 1, :] = a[r] + b[c]
        top = _top_rows(cand_scr[...], K + 1)
        z = jnp.ones_like(top[0])
        for r in range(1, K):
            z = z + jnp.exp(top[r] - top[0])
        tau = 0.5 * (top[K - 1] + top[K])
        thr_ref[h] = tau - s[0]
        w0_ref[h] = 0.5 * jnp.exp(s[0] - a[0]) / z
        s1_ref[h] = s[1]
        e1_ref[h] = jnp.exp(s[1] - b[0])


def _route(u2, wq_t, keys, tb):
    t, d = u2.shape
    n_pairs = sum(1 for r in range(P_TOPK + 1) for c in range(P_TOPK + 1)
                  if (r + 1) * (c + 1) <= P_TOPK + 1)
    n_cand = -(-n_pairs // SUBLANES) * SUBLANES
    out = jax.ShapeDtypeStruct((P_HEADS, P_NKEYS, t), F32)
    ospec = pl.BlockSpec((P_HEADS, P_NKEYS, tb), lambda i: (0, 0, i))
    return pl.pallas_call(
        _route_kernel,
        out_shape=(out, out, out, out),
        grid=(t // tb,),
        in_specs=[pl.BlockSpec((tb, d), lambda i: (i, 0)),
                  pl.BlockSpec(wq_t.shape, lambda i: (0, 0)),
                  pl.BlockSpec(keys.shape, lambda i: (0, 0, 0))],
        out_specs=(ospec, ospec, ospec, ospec),
        scratch_shapes=[pltpu.VMEM((n_cand, tb), F32)],
        compiler_params=_params("parallel"),
        name="route",
    )(u2, wq_t, keys)


PEER_JB = 32
PEER_IB = 4
PEER_TILES_PER_REGION = 2
MXU_TILE = 256
N_MXU = 2
MXU_ROWS = 64
ACC_ROWS_PER_ENTRY = 4


def _gate_tile(thr_ref, w0_ref, s1_ref, e1_ref, h_in, p_out, ib, tc, jb):
    lanes = slice(tc * LANES, (tc + 1) * LANES)
    jrows = slice(jb * PEER_JB, (jb + 1) * PEER_JB)
    gate = [None] * PEER_IB
    for h in range(P_HEADS):
        s1 = s1_ref[h, jrows, lanes]
        e1 = e1_ref[h, jrows, lanes]
        for k in range(PEER_IB):
            i = ib * PEER_IB + k
            term = (jnp.where(s1 >= thr_ref[h, i:i + 1, lanes], e1, 0.0)
                    * w0_ref[h, i:i + 1, lanes])
            gate[k] = term if gate[k] is None else gate[k] + term
    for k in range(PEER_IB):
        r0 = (ib * PEER_IB + k) * P_NKEYS + jb * PEER_JB
        hv = h_in[r0:r0 + PEER_JB, lanes]
        act = hv * (1.0 + lax.erf(hv * (2.0 ** -0.5)))
        p_out[r0:r0 + PEER_JB, lanes] = (act * gate[k]).astype(BF16)


def _peer_step(u2_ref, u_ref, vt_ref, thr_ref, w0_ref, s1_ref, e1_ref, acc_scr,
               h_in, h_out, p_in, p_out):
    f = pl.program_id(0)
    eb, d = u_ref.shape
    tb = u2_ref.shape[0]
    assert tb == N_MXU * MXU_TILE and eb == d
    n_kt = d // MXU_TILE
    n_c = eb // MXU_ROWS
    per_phase = n_kt * n_c
    tiles = [(ib, tc, jb) for ib in range(eb // (PEER_IB * P_NKEYS))
             for tc in range(tb // LANES) for jb in range(P_NKEYS // PEER_JB)]
    assert len(tiles) % PEER_TILES_PER_REGION == 0
    n_reg = len(tiles) // PEER_TILES_PER_REGION
    assert (2 * per_phase) % n_reg == 0
    per_reg = 2 * per_phase // n_reg

    def weights(w):
        phase, kt = divmod(w, n_kt)
        ks = slice(kt * MXU_TILE, (kt + 1) * MXU_TILE)
        for q in range(N_MXU):
            ts = slice(q * MXU_TILE, (q + 1) * MXU_TILE)
            if phase == 0:
                pltpu.matmul_push_rhs(u2_ref[ts, ks], staging_register=w % 2, mxu_index=q,
                                      transpose=True)
            else:
                pltpu.matmul_push_rhs(p_in[ks, ts], staging_register=w % 2, mxu_index=q)

    def piece(idx):
        w, c = divmod(idx, n_c)
        phase, kt = divmod(w, n_kt)
        rows = slice(c * MXU_ROWS, (c + 1) * MXU_ROWS)
        ks = slice(kt * MXU_TILE, (kt + 1) * MXU_TILE)
        addr = c * (MXU_ROWS // ACC_ROWS_PER_ENTRY)
        if c == n_c // 2 and w + 1 < 2 * n_kt:
            weights(w + 1)
        for q in range(N_MXU):
            ts = slice(q * MXU_TILE, (q + 1) * MXU_TILE)
            if kt == 0 and phase == 0:
                acc_scr[rows, ts] += pltpu.matmul_pop(addr, (MXU_ROWS, MXU_TILE), F32, mxu_index=q)
            if kt == 0 and phase == 1:
                h_out[rows, ts] = pltpu.matmul_pop(addr, (MXU_ROWS, MXU_TILE), F32, mxu_index=q)
            lhs = u_ref[rows, ks] if phase == 0 else vt_ref[rows, ks]
            pltpu.matmul_acc_lhs(addr, lhs, mxu_index=q,
                                 load_staged_rhs=(w % 2) if c == 0 else None)

    weights(0)
    for r in range(n_reg):
        @pl.when(f >= -r)
        def _(r=r):
            for idx in range(r * per_reg, (r + 1) * per_reg):
                piece(idx)
            for tile in tiles[r * PEER_TILES_PER_REGION:(r + 1) * PEER_TILES_PER_REGION]:
                _gate_tile(thr_ref, w0_ref, s1_ref, e1_ref, h_in, p_out, *tile)


def _peer_kernel(n_e, n_steps, u2_ref, x1_ref, fg_ref, u_ref, vt_ref, thr_ref, w0_ref, s1_ref,
                 e1_ref, out_ref, acc_scr, h_scr, p_scr):
    f = pl.program_id(0)
    full = (h_scr.shape[1], MXU_TILE)

    @pl.when(f == 0)
    def _():
        acc_scr[...] = jnp.zeros(acc_scr.shape, F32)
        p_scr[1] = jnp.zeros(p_scr.shape[1:], BF16)
        for q in range(N_MXU):
            h_scr[0, :, q * MXU_TILE:(q + 1) * MXU_TILE] = pltpu.matmul_pop(0, full, F32, mxu_index=q)
        h_scr[1] = jnp.zeros(h_scr.shape[1:], F32)

    step = functools.partial(_peer_step, u2_ref, u_ref, vt_ref, thr_ref, w0_ref, s1_ref, e1_ref,
                             acc_scr)

    @pl.when(f % 2 == 0)
    def _():
        step(h_scr.at[1], h_scr.at[0], p_scr.at[1], p_scr.at[0])

    @pl.when(f % 2 == 1)
    def _():
        step(h_scr.at[0], h_scr.at[1], p_scr.at[0], p_scr.at[1])

    @pl.when((f >= 3) & ((f - 3) % n_e == n_e - 1))
    def _():
        x2 = x1_ref[...] + acc_scr[...].T
        out_ref[...] = (x2 * lax.rsqrt(jnp.mean(x2 * x2, axis=-1, keepdims=True) + EPS)
                        * fg_ref[...])
        acc_scr[...] = jnp.zeros(acc_scr.shape, F32)

    @pl.when(f == n_steps - 1)
    def _():
        for q in range(N_MXU):
            h_scr[0, :, q * MXU_TILE:(q + 1) * MXU_TILE] = pltpu.matmul_pop(0, full, F32, mxu_index=q)


def _peer(u2, x1, fg, u_emb, v_t, thr, w0, s1, e1, tb, eb):
    t, d = u2.shape
    n_e = u_emb.shape[0] // eb
    n_i = eb // P_NKEYS
    n = (t // tb) * n_e
    n_steps = n + 3
    blk = lambda f, lag: jnp.clip(f - lag, 0, n - 1)
    tok = lambda lag: (lambda f: (blk(f, lag) // n_e, 0))
    gate_blk = lambda f: (0, blk(f, 1) % n_e, blk(f, 1) // n_e)
    key_blk = lambda f: (0, 0, blk(f, 1) // n_e)
    return pl.pallas_call(
        functools.partial(_peer_kernel, n_e, n_steps),
        out_shape=jax.ShapeDtypeStruct((t, d), F32),
        grid=(n_steps,),
        in_specs=[pl.BlockSpec((tb, d), tok(0)),
                  pl.BlockSpec((tb, d), tok(3)),
                  pl.BlockSpec((1, d), lambda f: (0, 0)),
                  pl.BlockSpec((eb, d), lambda f: (blk(f, 0) % n_e, 0)),
                  pl.BlockSpec((d, eb), lambda f: (0, blk(f, 2) % n_e)),
                  pl.BlockSpec((P_HEADS, n_i, tb), gate_blk),
                  pl.BlockSpec((P_HEADS, n_i, tb), gate_blk),
                  pl.BlockSpec((P_HEADS, P_NKEYS, tb), key_blk),
                  pl.BlockSpec((P_HEADS, P_NKEYS, tb), key_blk)],
        out_specs=pl.BlockSpec((tb, d), tok(3)),
        scratch_shapes=[pltpu.VMEM((d, tb), F32), pltpu.VMEM((2, eb, tb), F32),
                        pltpu.VMEM((2, eb, tb), BF16)],
        compiler_params=_params("arbitrary"),
        name="peer",
    )(u2, x1, fg, u_emb, v_t, thr, w0, s1, e1)


def _tiles(t):
    pick = lambda pref: next(c for c in (pref, 1024, 512, 256, 128) if c <= pref and t % c == 0)
    return dict(inproj_tm=pick(2048), inproj_tn=1024, merge_tm=pick(512),
                route_tb=pick(256), peer_tb=pick(512), peer_eb=1024)


def kernel(x, norm1_g, w_in, b_in, conv_w, conv_b, m_head_g, lb_table, g_head_g, w_br_m, w_br_g,
           w_out, norm2_g, w_pq, sub_keys, u_emb, v_emb, final_g):
    nb, seq, d = x.shape
    t = nb * seq
    depth = w_in.shape[0]
    assert d == 1024 and seq % M_CHUNK == 0 and t % LANES == 0
    tl = _tiles(t)
    n_gate = 2 * M_HEADS
    g0 = 2 * M_QK + 2 * M_V
    assert (w_in.shape[2] - n_gate) % tl["inproj_tn"] == 0

    lb_all = jnp.cumsum(jax.nn.softmax(lb_table.astype(F32), axis=0), axis=0)
    x2 = x.reshape(t, d)
    row = lambda a: a.reshape(1, -1).astype(F32)

    for l in range(depth):
        w_main = jnp.concatenate([w_in[l][:, :g0], w_in[l][:, g0 + n_gate:]], axis=1).astype(BF16)
        b_main = jnp.concatenate([b_in[l][:g0], b_in[l][g0 + n_gate:]]).reshape(1, -1)
        w_gate = jnp.pad(w_in[l][:, g0:g0 + n_gate], ((0, 0), (0, LANES - n_gate))).astype(BF16)
        b_gate = jnp.pad(b_in[l][g0:g0 + n_gate], (0, LANES - n_gate)).reshape(1, -1)

        proj, gates = _inproj(x2, row(norm1_g[l]), w_main, b_main, w_gate, b_gate,
                              tl["inproj_tm"], tl["inproj_tn"])
        y_m = _mlstm(proj, gates, conv_w[l], row(conv_b[l]), row(m_head_g[l]),
                     nb, seq // M_CHUNK)
        y_g = _hgrn(proj, row(lb_all[l]), row(g_head_g[l]), nb, seq // G_CHUNK, g0 // G_W)
        x1, u2 = _merge(x2, y_m, y_g, proj, w_br_m[l].astype(BF16), w_br_g[l].astype(BF16),
                        w_out[l].astype(BF16), row(norm2_g[l]), tl["merge_tm"],
                        (g0 + 4 * G_W) // d)

        keys = sub_keys[l].reshape(2 * P_HEADS, P_NKEYS, -1).astype(BF16)
        thr, w0, s1, e1 = _route(u2, w_pq[l].T.astype(BF16), keys, tl["route_tb"])
        assert depth == 1
        x2 = _peer(u2, x1, row(final_g), u_emb[l].astype(BF16), v_emb[l].T.astype(BF16),
                   thr, w0, s1, e1, tl["peer_tb"], tl["peer_eb"])
    return x2.reshape(nb, seq, d)
```

```python
import functools

import jax
import jax.numpy as jnp
from jax import lax
from jax.experimental import pallas as pl
from jax.experimental.pallas import tpu as pltpu

F32 = jnp.float32
BF16 = jnp.bfloat16
HIGHEST = lax.Precision.HIGHEST

EPS = 1e-6
LOG2E = 1.4426950408889634
LANES = 128
SUBLANES = 8
VMEM_LIMIT = 48 * 1024 * 1024

M_HEADS, M_QK_DIM, M_V_DIM, M_CHUNK, CONV_W = 4, 256, 512, 128, 4
M_CHUNKS_PER_STEP = 8
G_HEADS, G_DIM, G_CHUNK, G_SUB = 8, 128, 64, 16
G_CHUNKS_PER_STEP = 8
P_HEADS, P_NKEYS, P_TOPK = 8, 128, 16
M_QK = M_HEADS * M_QK_DIM
M_V = M_HEADS * M_V_DIM
G_W = G_HEADS * G_DIM


def _params(*sem):
    return pltpu.CompilerParams(dimension_semantics=sem, vmem_limit_bytes=VMEM_LIMIT)


def _dot(a, b):
    return jnp.dot(a, b, preferred_element_type=F32)


def _dot_nt(a, b):
    return lax.dot_general(a, b, (((1,), (1,)), ((), ())), preferred_element_type=F32)


def _dot_tn(a, b):
    return lax.dot_general(a, b, (((0,), (0,)), ((), ())), preferred_element_type=F32)


def _sigmoid(x):
    return 1.0 / (1.0 + jnp.exp2(x * (-LOG2E)))


def _tril(n):
    r = lax.broadcasted_iota(jnp.int32, (n, n), 0)
    c = lax.broadcasted_iota(jnp.int32, (n, n), 1)
    return (r >= c)


def _inproj_kernel(x_ref, g_ref, w_ref, b_ref, wg_ref, bg_ref, out_ref, gate_ref, u_scr):
    @pl.when(pl.program_id(1) == 0)
    def _():
        x = x_ref[...]
        u = x * lax.rsqrt(jnp.mean(x * x, axis=-1, keepdims=True) + EPS) * g_ref[...]
        u_scr[...] = u.astype(BF16)
        gate_ref[...] = _dot(u.astype(BF16), wg_ref[...]) + bg_ref[...]

    out_ref[...] = (_dot(u_scr[...], w_ref[...]) + b_ref[...]).astype(BF16)


def _inproj(x2, g, w, b, wg, bg, tm, tn):
    t, d = x2.shape
    n = w.shape[1]
    return pl.pallas_call(
        _inproj_kernel,
        out_shape=(jax.ShapeDtypeStruct((t, n), BF16), jax.ShapeDtypeStruct((t, LANES), F32)),
        grid=(t // tm, n // tn),
        in_specs=[
            pl.BlockSpec((tm, d), lambda i, j: (i, 0)),
            pl.BlockSpec((1, d), lambda i, j: (0, 0)),
            pl.BlockSpec((d, tn), lambda i, j: (0, j)),
            pl.BlockSpec((1, tn), lambda i, j: (0, j)),
            pl.BlockSpec((d, LANES), lambda i, j: (0, 0)),
            pl.BlockSpec((1, LANES), lambda i, j: (0, 0)),
        ],
        out_specs=(pl.BlockSpec((tm, tn), lambda i, j: (i, j)),
                   pl.BlockSpec((tm, LANES), lambda i, j: (i, 0))),
        scratch_shapes=[pltpu.VMEM((tm, d), BF16)],
        compiler_params=_params("parallel", "arbitrary"),
        name="inproj",
    )(x2, g, w, b, wg, bg)


def _mlstm_kernel(qk_ref, v_ref, o_ref, gt_ref, cw_ref, cb_ref, hg_ref, y_ref,
                  ext_scr, c_scr, m_scr):
    L = M_CHUNK

    @pl.when(pl.program_id(1) == 0)
    def _():
        ext_scr[0:SUBLANES, :] = jnp.zeros((SUBLANES, 2 * M_QK), F32)
        c_scr[...] = jnp.zeros(c_scr.shape, F32)
        m_scr[...] = jnp.zeros(m_scr.shape, F32)

    for cc in range(qk_ref.shape[0] // L):
        _mlstm_chunk(slice(cc * L, (cc + 1) * L), qk_ref, v_ref, o_ref, gt_ref, cw_ref, cb_ref, hg_ref,
                     y_ref, ext_scr, c_scr, m_scr)


def _mlstm_chunk(rows, qk_ref, v_ref, o_ref, gt_ref, cw_ref, cb_ref, hg_ref, y_ref,
                 ext_scr, c_scr, m_scr):
    L = M_CHUNK
    cur = qk_ref[rows, :].astype(F32)
    ext_scr[SUBLANES:SUBLANES + L, :] = cur
    acc = cb_ref[...] + cw_ref[CONV_W - 1:CONV_W, :] * cur
    for j in range(CONV_W - 1):
        off = SUBLANES - (CONV_W - 1) + j
        acc = acc + cw_ref[j:j + 1, :] * ext_scr[off:off + L, :]
    ext_scr[0:SUBLANES, :] = cur[L - SUBLANES:L, :]
    qk = acc * _sigmoid(acc)

    gt = gt_ref[rows, :]
    lf = jnp.minimum(gt, 0.0) - jnp.log(1.0 + jnp.exp(-jnp.abs(gt)))
    tri = _tril(L)
    bcols = jnp.dot(tri.astype(F32), lf, precision=HIGHEST, preferred_element_type=F32)
    gt_t = gt.T
    b_t = bcols.T
    ones_ext = jnp.ones((L, LANES), BF16)

    for h in range(M_HEADS):
        icol = gt[:, h:h + 1]
        bcol = bcols[:, M_HEADS + h:M_HEADS + h + 1]
        irow = gt_t[h:h + 1, :]
        brow = b_t[M_HEADS + h:M_HEADS + h + 1, :]
        m_prev = m_scr[h, 0:1, 0:1]
        b_last = bcol[L - 1:L, :]

        dmat = jnp.where(tri, bcol - brow + irow, -jnp.inf)
        inter = bcol + m_prev
        m_t = jnp.maximum(inter, jnp.max(dmat, axis=-1, keepdims=True))
        dexp = jnp.exp(dmat - m_t)
        inter_w = jnp.exp(inter - m_t)

        q = (qk[:, h * M_QK_DIM:(h + 1) * M_QK_DIM] * (M_QK_DIM ** -0.5)).astype(BF16)
        kf = qk[:, M_QK + h * M_QK_DIM:M_QK + (h + 1) * M_QK_DIM]
        v_ext = jnp.concatenate([v_ref[rows, h * M_V_DIM:(h + 1) * M_V_DIM], ones_ext], axis=1)

        scores = (_dot_nt(q, kf.astype(BF16)) * dexp).astype(BF16)
        c_old = c_scr[h]
        nd = _dot(scores, v_ext) + inter_w * _dot(q, c_old.astype(BF16))
        den = nd[:, M_V_DIM:]
        inv = 1.0 / jnp.maximum(jnp.abs(den), jnp.exp(-m_t))
        hs = [nd[:, j * LANES:(j + 1) * LANES] * inv for j in range(M_V_DIM // LANES)]
        hh = jnp.concatenate(hs, axis=1)

        m_new = m_t[L - 1:L, :]
        w_s = jnp.exp(b_last - bcol + icol - m_new)
        decay = jnp.exp(b_last + m_prev - m_new)
        kw = (kf * w_s).astype(BF16)
        c_scr[h] = decay * c_old + _dot_tn(kw, v_ext)
        m_scr[h] = jnp.broadcast_to(m_new, (SUBLANES, LANES))

        hn = hh * lax.rsqrt(jnp.mean(hh * hh, axis=-1, keepdims=True) + EPS)
        sl = slice(h * M_V_DIM, (h + 1) * M_V_DIM)
        og = _sigmoid(o_ref[rows, sl].astype(F32))
        y_ref[rows, sl] = (og * hn * hg_ref[:, sl]).astype(BF16)


def _mlstm(proj, gates, conv_w, conv_b, head_g, nb, nc):
    t = proj.shape[0]
    L = M_CHUNK * M_CHUNKS_PER_STEP
    nc = nc // M_CHUNKS_PER_STEP
    row = lambda b, c: b * nc + c
    return pl.pallas_call(
        _mlstm_kernel,
        out_shape=jax.ShapeDtypeStruct((t, M_V), BF16),
        grid=(nb, nc),
        in_specs=[
            pl.BlockSpec((L, 2 * M_QK), lambda b, c: (row(b, c), 0)),
            pl.BlockSpec((L, M_V), lambda b, c: (row(b, c), 1)),
            pl.BlockSpec((L, M_V), lambda b, c: (row(b, c), 2)),
            pl.BlockSpec((L, LANES), lambda b, c: (row(b, c), 0)),
            pl.BlockSpec((CONV_W, 2 * M_QK), lambda b, c: (0, 0)),
            pl.BlockSpec((1, 2 * M_QK), lambda b, c: (0, 0)),
            pl.BlockSpec((1, M_V), lambda b, c: (0, 0)),
        ],
        out_specs=pl.BlockSpec((L, M_V), lambda b, c: (row(b, c), 0)),
        scratch_shapes=[
            pltpu.VMEM((SUBLANES + M_CHUNK, 2 * M_QK), F32),
            pltpu.VMEM((M_HEADS, M_QK_DIM, M_V_DIM + LANES), F32),
            pltpu.VMEM((M_HEADS, SUBLANES, LANES), F32),
        ],
        compiler_params=_params("parallel", "arbitrary"),
        name="mlstm",
    )(proj, proj, proj, gates, conv_w, conv_b, head_g)


def _hgrn_kernel(gf_ref, gi_ref, gq_ref, gg_ref, lb_ref, hg_ref, y_ref, st_scr):
    L, SUB, NSUB = G_CHUNK, G_SUB, G_CHUNK // G_SUB

    @pl.when(pl.program_id(1) == 0)
    def _():
        st_scr[...] = jnp.zeros(st_scr.shape, F32)

    lb = lb_ref[...]
    for cc in range(gf_ref.shape[0] // L):
        _hgrn_chunk(slice(cc * L, (cc + 1) * L), lb, gf_ref, gi_ref, gq_ref, gg_ref, hg_ref, y_ref, st_scr)


def _hgrn_chunk(rows, lb, gf_ref, gi_ref, gq_ref, gg_ref, hg_ref, y_ref, st_scr):
    L, SUB, NSUB = G_CHUNK, G_SUB, G_CHUNK // G_SUB
    f = lb + (1.0 - lb) * _sigmoid(gf_ref[rows, :].astype(F32))
    k_all = 1.0 - f
    cum_all = jnp.dot(_tril(L).astype(F32), jnp.log(f), precision=HIGHEST,
                      preferred_element_type=F32)
    gq = gq_ref[rows, :].astype(F32)
    q_all = gq * _sigmoid(gq)

    row = lax.broadcasted_iota(jnp.int32, (L, G_DIM), 0)
    row_sub = lax.broadcasted_iota(jnp.int32, (SUB, G_DIM), 0)
    ones_sq = jnp.ones((G_DIM, G_DIM), BF16)

    for h in range(G_HEADS):
        sl = slice(h * G_DIM, (h + 1) * G_DIM)
        cum, q, k = cum_all[:, sl], q_all[:, sl], k_all[:, sl]
        iv = gi_ref[rows, sl]
        ivf = iv.astype(F32)
        ends = [cum[SUB * b + SUB - 1:SUB * b + SUB, :] for b in range(NSUB)]
        last = ends[NSUB - 1]
        end_of_row = jnp.concatenate([jnp.broadcast_to(e, (SUB, G_DIM)) for e in ends], axis=0)

        khat = k * jnp.exp(end_of_row - cum)
        qs, ks = [], []
        for b in range(NSUB - 1):
            lo = SUB * (b + 1)
            qs.append(jnp.where(row >= lo, q * jnp.exp(jnp.minimum(cum - ends[b], 0.0)), 0.0))
            ks.append(jnp.where((row >= lo - SUB) & (row < lo), khat, 0.0))
        att = _dot_nt(jnp.concatenate(qs, axis=1).astype(BF16),
                      jnp.concatenate(ks, axis=1).astype(BF16))
        o = _dot(att.astype(BF16), iv)

        st = st_scr[h]
        o = o + _dot_nt((q * jnp.exp(cum)).astype(BF16), st.astype(BF16))

        o_diag = []
        H = SUBLANES
        for a in range(NSUB):
            r0 = SUB * a
            q_a, k_a, c_a, i_a = (z[r0:r0 + SUB, :] for z in (q, k, cum, ivf))
            ms = []
            for s in range(SUB):
                lo = 0 if s < H else H
                diff = c_a[lo:, :] - c_a[s:s + 1, :]
                part = jnp.where(row_sub[lo:lo + H, :] >= s, diff[0:H, :], -jnp.inf)
                diff = jnp.concatenate([part, diff[H:, :]], axis=0) if lo == 0 else part
                ms.append(q_a[lo:, :] * (k_a[s:s + 1, :] * jnp.exp(diff)))
            red = _dot(jnp.concatenate(ms, axis=0).astype(BF16), ones_sq)
            acc_lo = red[0:H, :] * i_a[0:1, :]
            acc_hi = red[H:SUB, :] * i_a[0:1, :]
            for s in range(1, H):
                acc_lo = acc_lo + red[s * SUB:s * SUB + H, :] * i_a[s:s + 1, :]
                acc_hi = acc_hi + red[s * SUB + H:(s + 1) * SUB, :] * i_a[s:s + 1, :]
            for s in range(H, SUB):
                off = H * SUB + (s - H) * H
                acc_hi = acc_hi + red[off:off + H, :] * i_a[s:s + 1, :]
            o_diag += [acc_lo, acc_hi]
        o = o + jnp.concatenate(o_diag, axis=0)

        kd = (k * jnp.exp(last - cum)).astype(BF16)
        st_scr[h] = jnp.exp(last) * st + _dot_tn(iv, kd)

        hn = o * lax.rsqrt(jnp.mean(o * o, axis=-1, keepdims=True) + EPS)
        gg = gg_ref[rows, sl].astype(F32)
        y_ref[rows, sl] = (hn * hg_ref[:, sl] * (gg * _sigmoid(gg))).astype(BF16)


def _hgrn(proj, lb, head_g, nb, nc, col0):
    t = proj.shape[0]
    L = G_CHUNK * G_CHUNKS_PER_STEP
    nc = nc // G_CHUNKS_PER_STEP
    row = lambda b, c: b * nc + c
    spec = lambda j: pl.BlockSpec((L, G_W), lambda b, c: (row(b, c), col0 + j))
    return pl.pallas_call(
        _hgrn_kernel,
        out_shape=jax.ShapeDtypeStruct((t, G_W), BF16),
        grid=(nb, nc),
        in_specs=[spec(0), spec(1), spec(2), spec(3),
                  pl.BlockSpec((1, G_W), lambda b, c: (0, 0)),
                  pl.BlockSpec((1, G_W), lambda b, c: (0, 0))],
        out_specs=pl.BlockSpec((L, G_W), lambda b, c: (row(b, c), 0)),
        scratch_shapes=[pltpu.VMEM((G_HEADS, G_DIM, G_DIM), F32)],
        compiler_params=_params("parallel", "arbitrary"),
        name="hgrn2",
    )(proj, proj, proj, proj, lb, head_g)


def _merge_kernel(x_ref, ym_ref, yg_ref, am_ref, ag_ref, wm_ref, wg_ref, wo_ref, g2_ref,
                  x1_ref, u2_ref):
    zm = _dot(ym_ref[...], wm_ref[...])
    zg = _dot(yg_ref[...], wg_ref[...])
    z = _sigmoid(am_ref[...].astype(F32)) * zm + _sigmoid(ag_ref[...].astype(F32)) * zg
    x1 = x_ref[...] + _dot(z.astype(BF16), wo_ref[...])
    x1_ref[...] = x1
    u2 = x1 * lax.rsqrt(jnp.mean(x1 * x1, axis=-1, keepdims=True) + EPS) * g2_ref[...]
    u2_ref[...] = u2.astype(BF16)


def _merge(x2, ym, yg, proj, wm, wg, wo, g2, tm, col_am):
    t, d = x2.shape
    rows = lambda w: pl.BlockSpec((tm, w), lambda i: (i, 0))
    full = lambda a: pl.BlockSpec(a.shape, lambda i: (0, 0))
    return pl.pallas_call(
        _merge_kernel,
        out_shape=(jax.ShapeDtypeStruct((t, d), F32), jax.ShapeDtypeStruct((t, d), BF16)),
        grid=(t // tm,),
        in_specs=[rows(d), rows(M_V), rows(G_W),
                  pl.BlockSpec((tm, d), lambda i: (i, col_am)),
                  pl.BlockSpec((tm, d), lambda i: (i, col_am + 1)),
                  full(wm), full(wg), full(wo), full(g2)],
        out_specs=(rows(d), rows(d)),
        compiler_params=_params("parallel"),
        name="merge",
    )(x2, ym, yg, proj, proj, wm, wg, wo, g2)


def _sort16_pairs():
    def merge(lo, hi, r):
        step = r * 2
        if step < hi - lo:
            yield from merge(lo, hi, step)
            yield from merge(lo + r, hi, step)
            yield from [(i, i + r) for i in range(lo + r, hi - r, step)]
        else:
            yield (lo, lo + r)

    def sort(lo, hi):
        if hi - lo >= 1:
            mid = lo + (hi - lo) // 2
            yield from sort(lo, mid)
            yield from sort(mid + 1, hi)
            yield from merge(lo, hi, 1)

    return list(sort(0, P_TOPK - 1))


def _top_rows(vals, k):
    out = []
    for _ in range(k):
        m = jnp.max(vals, axis=0, keepdims=True)
        out.append(m)
        vals = jnp.where(vals == m, -jnp.inf, vals)
    return out


def _top_of_128(vals, k):
    n = P_TOPK
    v = [vals[SUBLANES * g:SUBLANES * (g + 1), :] for g in range(n)]
    for a, b in _sort16_pairs():
        v[a], v[b] = jnp.maximum(v[a], v[b]), jnp.minimum(v[a], v[b])
    out = []
    for r in range(k):
        m = jnp.max(v[0], axis=0, keepdims=True)
        out.append(m)
        if r == k - 1:
            break
        hit = v[0] == m
        depth = k - 1 - r
        for i in range(min(depth, n - 1)):
            v[i] = jnp.where(hit, v[i + 1], v[i])
        if depth >= n:
            v[n - 1] = jnp.where(hit, -jnp.inf, v[n - 1])
    return out


def _route_kernel(u2_ref, wq_ref, keys_ref, thr_ref, w0_ref, s1_ref, e1_ref, cand_scr):
    K = P_TOPK
    tb = u2_ref.shape[0]
    q_t = _dot_nt(wq_ref[...], u2_ref[...])
    pairs = [(r, c) for r in range(K + 1) for c in range(K + 1) if (r + 1) * (c + 1) <= K + 1]
    n_pad = cand_scr.shape[0] - len(pairs)
    cand_scr[len(pairs):, :] = jnp.full((n_pad, tb), -jnp.inf, F32)

    for h in range(P_HEADS):
        s = []
        for p in range(2):
            r0 = (2 * h + p) * P_NKEYS
            s.append(_dot(keys_ref[2 * h + p], q_t[r0:r0 + P_NKEYS, :].astype(BF16)))
        a = _top_of_128(s[0], K + 1)
        b = _top_of_128(s[1], K + 1)
        for n, (r, c) in enumerate(pairs):
            cand_scr[n:n + 1, :] = a[r] + b[c]
        top = _top_rows(cand_scr[...], K + 1)
        z = jnp.ones_like(top[0])
        for r in range(1, K):
            z = z + jnp.exp(top[r] - top[0])
        tau = 0.5 * (top[K - 1] + top[K])
        thr_ref[h] = tau - s[0]
        w0_ref[h] = 0.5 * jnp.exp(s[0] - a[0]) / z
        s1_ref[h] = s[1]
        e1_ref[h] = jnp.exp(s[1] - b[0])


def _route(u2, wq_t, keys, tb):
    t, d = u2.shape
    n_pairs = sum(1 for r in range(P_TOPK + 1) for c in range(P_TOPK + 1)
                  if (r + 1) * (c + 1) <= P_TOPK + 1)
    n_cand = -(-n_pairs // SUBLANES) * SUBLANES
    out = jax.ShapeDtypeStruct((P_HEADS, P_NKEYS, t), F32)
    ospec = pl.BlockSpec((P_HEADS, P_NKEYS, tb), lambda i: (0, 0, i))
    return pl.pallas_call(
        _route_kernel,
        out_shape=(out, out, out, out),
        grid=(t // tb,),
        in_specs=[pl.BlockSpec((tb, d), lambda i: (i, 0)),
                  pl.BlockSpec(wq_t.shape, lambda i: (0, 0)),
                  pl.BlockSpec(keys.shape, lambda i: (0, 0, 0))],
        out_specs=(ospec, ospec, ospec, ospec),
        scratch_shapes=[pltpu.VMEM((n_cand, tb), F32)],
        compiler_params=_params("parallel"),
        name="route",
    )(u2, wq_t, keys)


PEER_JB = 32
PEER_IB = 4
PEER_TILES_PER_REGION = 2
MXU_TILE = 256
N_MXU = 2
MXU_ROWS = 64
ACC_ROWS_PER_ENTRY = 4


def _gate_tile(thr_ref, w0_ref, s1_ref, e1_ref, h_in, p_out, ib, tc, jb):
    lanes = slice(tc * LANES, (tc + 1) * LANES)
    jrows = slice(jb * PEER_JB, (jb + 1) * PEER_JB)
    gate = [None] * PEER_IB
    for h in range(P_HEADS):
        s1 = s1_ref[h, jrows, lanes]
        e1 = e1_ref[h, jrows, lanes]
        for k in range(PEER_IB):
            i = ib * PEER_IB + k
            term = (jnp.where(s1 >= thr_ref[h, i:i + 1, lanes], e1, 0.0)
                    * w0_ref[h, i:i + 1, lanes])
            gate[k] = term if gate[k] is None else gate[k] + term
    for k in range(PEER_IB):
        r0 = (ib * PEER_IB + k) * P_NKEYS + jb * PEER_JB
        hv = h_in[r0:r0 + PEER_JB, lanes]
        act = hv * (1.0 + lax.erf(hv * (2.0 ** -0.5)))
        p_out[r0:r0 + PEER_JB, lanes] = (act * gate[k]).astype(BF16)


def _peer_step(u2_ref, u_ref, vt_ref, thr_ref, w0_ref, s1_ref, e1_ref, acc_scr,
               h_in, h_out, p_in, p_out):
    f = pl.program_id(0)
    eb, d = u_ref.shape
    tb = u2_ref.shape[0]
    assert tb == N_MXU * MXU_TILE and eb == d
    n_kt = d // MXU_TILE
    n_c = eb // MXU_ROWS
    per_phase = n_kt * n_c
    tiles = [(ib, tc, jb) for ib in range(eb // (PEER_IB * P_NKEYS))
             for tc in range(tb // LANES) for jb in range(P_NKEYS // PEER_JB)]
    assert len(tiles) % PEER_TILES_PER_REGION == 0
    n_reg = len(tiles) // PEER_TILES_PER_REGION
    assert (2 * per_phase) % n_reg == 0
    per_reg = 2 * per_phase // n_reg

    def weights(w):
        phase, kt = divmod(w, n_kt)
        ks = slice(kt * MXU_TILE, (kt + 1) * MXU_TILE)
        for q in range(N_MXU):
            ts = slice(q * MXU_TILE, (q + 1) * MXU_TILE)
            if phase == 0:
                pltpu.matmul_push_rhs(u2_ref[ts, ks], staging_register=w % 2, mxu_index=q,
                                      transpose=True)
            else:
                pltpu.matmul_push_rhs(p_in[ks, ts], staging_register=w % 2, mxu_index=q)

    def piece(idx):
        w, c = divmod(idx, n_c)
        phase, kt = divmod(w, n_kt)
        rows = slice(c * MXU_ROWS, (c + 1) * MXU_ROWS)
        ks = slice(kt * MXU_TILE, (kt + 1) * MXU_TILE)
        addr = c * (MXU_ROWS // ACC_ROWS_PER_ENTRY)
        if c == n_c // 2 and w + 1 < 2 * n_kt:
            weights(w + 1)
        for q in range(N_MXU):
            ts = slice(q * MXU_TILE, (q + 1) * MXU_TILE)
            if kt == 0 and phase == 0:
                acc_scr[rows, ts] += pltpu.matmul_pop(addr, (MXU_ROWS, MXU_TILE), F32, mxu_index=q)
            if kt == 0 and phase == 1:
                h_out[rows, ts] = pltpu.matmul_pop(addr, (MXU_ROWS, MXU_TILE), F32, mxu_index=q)
            lhs = u_ref[rows, ks] if phase == 0 else vt_ref[rows, ks]
            pltpu.matmul_acc_lhs(addr, lhs, mxu_index=q,
                                 load_staged_rhs=(w % 2) if c == 0 else None)

    weights(0)
    for r in range(n_reg):
        @pl.when(f >= -r)
        def _(r=r):
            for idx in range(r * per_reg, (r + 1) * per_reg):
                piece(idx)
            for tile in tiles[r * PEER_TILES_PER_REGION:(r + 1) * PEER_TILES_PER_REGION]:
                _gate_tile(thr_ref, w0_ref, s1_ref, e1_ref, h_in, p_out, *tile)


def _peer_kernel(n_e, n_steps, u2_ref, x1_ref, fg_ref, u_ref, vt_ref, thr_ref, w0_ref, s1_ref,
                 e1_ref, out_ref, acc_scr, h_scr, p_scr):
    f = pl.program_id(0)
    full = (h_scr.shape[1], MXU_TILE)

    @pl.when(f == 0)
    def _():
        acc_scr[...] = jnp.zeros(acc_scr.shape, F32)
        p_scr[1] = jnp.zeros(p_scr.shape[1:], BF16)
        for q in range(N_MXU):
            h_scr[0, :, q * MXU_TILE:(q + 1) * MXU_TILE] = pltpu.matmul_pop(0, full, F32, mxu_index=q)
        h_scr[1] = jnp.zeros(h_scr.shape[1:], F32)

    step = functools.partial(_peer_step, u2_ref, u_ref, vt_ref, thr_ref, w0_ref, s1_ref, e1_ref,
                             acc_scr)

    @pl.when(f % 2 == 0)
    def _():
        step(h_scr.at[1], h_scr.at[0], p_scr.at[1], p_scr.at[0])

    @pl.when(f % 2 == 1)
    def _():
        step(h_scr.at[0], h_scr.at[1], p_scr.at[0], p_scr.at[1])

    @pl.when((f >= 3) & ((f - 3) % n_e == n_e - 1))
    def _():
        x2 = x1_ref[...] + acc_scr[...].T
        out_ref[...] = (x2 * lax.rsqrt(jnp.mean(x2 * x2, axis=-1, keepdims=True) + EPS)
                        * fg_ref[...])
        acc_scr[...] = jnp.zeros(acc_scr.shape, F32)

    @pl.when(f == n_steps - 1)
    def _():
        for q in range(N_MXU):
            h_scr[0, :, q * MXU_TILE:(q + 1) * MXU_TILE] = pltpu.matmul_pop(0, full, F32, mxu_index=q)


def _peer(u2, x1, fg, u_emb, v_t, thr, w0, s1, e1, tb, eb):
    t, d = u2.shape
    n_e = u_emb.shape[0] // eb
    n_i = eb // P_NKEYS
    n = (t // tb) * n_e
    n_steps = n + 3
    blk = lambda f, lag: jnp.clip(f - lag, 0, n - 1)
    tok = lambda lag: (lambda f: (blk(f, lag) // n_e, 0))
    gate_blk = lambda f: (0, blk(f, 1) % n_e, blk(f, 1) // n_e)
    key_blk = lambda f: (0, 0, blk(f, 1) // n_e)
    return pl.pallas_call(
        functools.partial(_peer_kernel, n_e, n_steps),
        out_shape=jax.ShapeDtypeStruct((t, d), F32),
        grid=(n_steps,),
        in_specs=[pl.BlockSpec((tb, d), tok(0)),
                  pl.BlockSpec((tb, d), tok(3)),
                  pl.BlockSpec((1, d), lambda f: (0, 0)),
                  pl.BlockSpec((eb, d), lambda f: (blk(f, 0) % n_e, 0)),
                  pl.BlockSpec((d, eb), lambda f: (0, blk(f, 2) % n_e)),
                  pl.BlockSpec((P_HEADS, n_i, tb), gate_blk),
                  pl.BlockSpec((P_HEADS, n_i, tb), gate_blk),
                  pl.BlockSpec((P_HEADS, P_NKEYS, tb), key_blk),
                  pl.BlockSpec((P_HEADS, P_NKEYS, tb), key_blk)],
        out_specs=pl.BlockSpec((tb, d), tok(3)),
        scratch_shapes=[pltpu.VMEM((d, tb), F32), pltpu.VMEM((2, eb, tb), F32),
                        pltpu.VMEM((2, eb, tb), BF16)],
        compiler_params=_params("arbitrary"),
        name="peer",
    )(u2, x1, fg, u_emb, v_t, thr, w0, s1, e1)


def _tiles(t):
    pick = lambda pref: next(c for c in (pref, 1024, 512, 256, 128) if c <= pref and t % c == 0)
    return dict(inproj_tm=pick(2048), inproj_tn=1024, merge_tm=pick(512),
                route_tb=pick(256), peer_tb=pick(512), peer_eb=1024)


def kernel(x, norm1_g, w_in, b_in, conv_w, conv_b, m_head_g, lb_table, g_head_g, w_br_m, w_br_g,
           w_out, norm2_g, w_pq, sub_keys, u_emb, v_emb, final_g):
    nb, seq, d = x.shape
    t = nb * seq
    depth = w_in.shape[0]
    assert d == 1024 and seq % M_CHUNK == 0 and t % LANES == 0
    tl = _tiles(t)
    n_gate = 2 * M_HEADS
    g0 = 2 * M_QK + 2 * M_V
    assert (w_in.shape[2] - n_gate) % tl["inproj_tn"] == 0

    lb_all = jnp.cumsum(jax.nn.softmax(lb_table.astype(F32), axis=0), axis=0)
    x2 = x.reshape(t, d)
    row = lambda a: a.reshape(1, -1).astype(F32)

    for l in range(depth):
        w_main = jnp.concatenate([w_in[l][:, :g0], w_in[l][:, g0 + n_gate:]], axis=1).astype(BF16)
        b_main = jnp.concatenate([b_in[l][:g0], b_in[l][g0 + n_gate:]]).reshape(1, -1)
        w_gate = jnp.pad(w_in[l][:, g0:g0 + n_gate], ((0, 0), (0, LANES - n_gate))).astype(BF16)
        b_gate = jnp.pad(b_in[l][g0:g0 + n_gate], (0, LANES - n_gate)).reshape(1, -1)

        proj, gates = _inproj(x2, row(norm1_g[l]), w_main, b_main, w_gate, b_gate,
                              tl["inproj_tm"], tl["inproj_tn"])
        y_m = _mlstm(proj, gates, conv_w[l], row(conv_b[l]), row(m_head_g[l]),
                     nb, seq // M_CHUNK)
        y_g = _hgrn(proj, row(lb_all[l]), row(g_head_g[l]), nb, seq // G_CHUNK, g0 // G_W)
        x1, u2 = _merge(x2, y_m, y_g, proj, w_br_m[l].astype(BF16), w_br_g[l].astype(BF16),
                        w_out[l].astype(BF16), row(norm2_g[l]), tl["merge_tm"],
                        (g0 + 4 * G_W) // d)

        keys = sub_keys[l].reshape(2 * P_HEADS, P_NKEYS, -1).astype(BF16)
        thr, w0, s1, e1 = _route(u2, w_pq[l].astype(BF16).T, keys, tl["route_tb"])
        assert depth == 1
        x2 = _peer(u2, x1, row(final_g), u_emb[l].astype(BF16), v_emb[l].astype(BF16).T,
                   thr, w0, s1, e1, tl["peer_tb"], tl["peer_eb"])
    return x2.reshape(nb, seq, d)
```
